```python
import math
import jax, jax.numpy as jnp
from jax import lax
import numpy as np

D_MODEL = 1024
BATCH = 1
SEQ = 16384
DEPTH = 2
DEC_BATCH = 8
DEC_SEQ = 8192
PAST_LEN = 128

EPS = 1e-6
N_MEM = 256

SSM_HEADS = 8
SSM_HEAD_DIM = 64
SSM_D = SSM_HEADS * SSM_HEAD_DIM
SSM_GROUPS = 2
SSM_HPG = SSM_HEADS // SSM_GROUPS
SSM_STATE = 128
D_CONV = 5
CONV_CH = SSM_D + 2 * SSM_GROUPS * SSM_STATE
CHUNK = 128
DT_MIN = 0.001
DT_MAX = 0.1

MLA_HEADS = 8
QK_NOPE = 64
QK_ROPE = 32
V_DIM = 64
Q_LORA = 256
KV_LORA = 128
ROPE_THETA = 10000.0
Q_BLOCK = 128
MLA_D = MLA_HEADS * V_DIM

D_IN_EVEN = SSM_D + CONV_CH + 2 * SSM_HEADS + Q_LORA + KV_LORA + QK_ROPE
D_MIX_EVEN = SSM_D + MLA_D

FOURIER_GROUPS = 4
FOURIER_GROUP_DIM = D_MODEL // FOURIER_GROUPS

XA_HEADS = 4
XA_HEAD_DIM = D_MODEL // XA_HEADS

D_FF = ((8 * D_MODEL + 3 * 256 - 1) // (3 * 256)) * 256

kernel_name = 'hybrid_ssd_mla_fnet_encoder'


def rmsnorm(x, w):
    xf = x.astype(jnp.float32)
    y = xf * lax.rsqrt(jnp.mean(xf * xf, axis=-1, keepdims=True) + EPS)
    return (y * w.astype(jnp.float32)).astype(x.dtype)


def rope_tables(s):
    inv = ROPE_THETA ** (-jnp.arange(0, QK_ROPE, 2, dtype=jnp.float32) / QK_ROPE)
    ang = jnp.arange(s, dtype=jnp.float32)[:, None] * inv[None, :]
    return jnp.cos(ang), jnp.sin(ang)


def apply_rope(x, cos, sin):
    half = x.shape[-1] // 2
    x1, x2 = x[..., :half], x[..., half:]
    return jnp.concatenate([x1 * cos - x2 * sin, x2 * cos + x1 * sin], axis=-1).astype(x.dtype)


def depthwise_conv(x, w, b):
    y = lax.conv_general_dilated(
        x, w[:, None, :].astype(x.dtype), window_strides=(1,),
        padding=[(D_CONV // 2, D_CONV // 2)],
        dimension_numbers=('NWC', 'WIO', 'NWC'),
        feature_group_count=x.shape[-1])
    return y + b.astype(x.dtype)


def ssd_chunked(x, dt, a, bm, cm):
    b, l = x.shape[0], x.shape[1]
    c = l // CHUNK
    xc = x.reshape(b, c, CHUNK, SSM_GROUPS, SSM_HPG, SSM_HEAD_DIM)
    dtc = dt.reshape(b, c, CHUNK, SSM_GROUPS, SSM_HPG)
    bc = bm.reshape(b, c, CHUNK, SSM_GROUPS, SSM_STATE)
    cc = cm.reshape(b, c, CHUNK, SSM_GROUPS, SSM_STATE)
    acs = jnp.cumsum(dtc * a, axis=2)
    xdt = xc * dtc[..., None]
    mask = jnp.tril(jnp.ones((CHUNK, CHUNK), dtype=bool))[:, :, None, None]
    seg = acs[:, :, :, None] - acs[:, :, None, :]
    decay = jnp.exp(jnp.where(mask, seg, -jnp.inf))
    cb = jnp.einsum('bclgn,bcsgn->bclsg', cc, bc)
    y_diag = jnp.einsum('bclsgr,bcsgrp->bclgrp', cb[..., None] * decay, xdt)
    decay_st = jnp.exp(acs[:, :, -1:] - acs)
    states = jnp.einsum('bclgn,bclgrp->bcgrpn', bc, xdt * decay_st[..., None])
    chunk_decay = jnp.exp(acs[:, :, -1])

    def step(h, inp):
        st, dec = inp
        return h * dec[..., None, None] + st, h

    h0 = jnp.zeros((b, SSM_GROUPS, SSM_HPG, SSM_HEAD_DIM, SSM_STATE), jnp.float32)
    _, prev = lax.scan(step, h0, (jnp.moveaxis(states, 1, 0), jnp.moveaxis(chunk_decay, 1, 0)))
    prev = jnp.moveaxis(prev, 0, 1)
    y_off = jnp.einsum('bclgn,bcgrpn->bclgrp', cc, prev) * jnp.exp(acs)[..., None]
    return (y_diag + y_off).reshape(b, l, SSM_GROUPS, SSM_HPG, SSM_HEAD_DIM)


def block_attention(q, k, v, scale):
    b, s, h, dk = q.shape
    nb = s // Q_BLOCK
    qb = jnp.moveaxis(q.reshape(b, nb, Q_BLOCK, h, dk), 1, 0)

    def attend(qblk):
        sc = jnp.einsum('bqhd,bkhd->bhqk', qblk, k).astype(jnp.float32) * scale
        p = jax.nn.softmax(sc, axis=-1).astype(v.dtype)
        return jnp.einsum('bhqk,bkhd->bqhd', p, v)

    o = lax.map(attend, qb)
    return jnp.moveaxis(o, 0, 1).reshape(b, s, h, v.shape[-1])


def even_mixer(hn, w_in, conv_w, conv_b, a_log_f, a_log_b, dt_bias_f, dt_bias_b, d_skip,
               ssm_norm_w, q_norm_w, w_uq, kv_norm_w, w_ukv, w_out):
    b, s, _ = hn.shape
    f32 = jnp.float32
    proj = hn @ w_in
    o1 = SSM_D
    o2 = o1 + CONV_CH
    o3 = o2 + 2 * SSM_HEADS
    o4 = o3 + Q_LORA
    o5 = o4 + KV_LORA
    z, xbc, dt_raw, c_q, c_kv, k_r = jnp.split(proj, [o1, o2, o3, o4, o5], axis=-1)

    xbc = jax.nn.silu(depthwise_conv(xbc, conv_w, conv_b))
    xs, bm, cm = jnp.split(xbc, [SSM_D, SSM_D + SSM_GROUPS * SSM_STATE], axis=-1)
    xs = xs.reshape(b, s, SSM_GROUPS, SSM_HPG, SSM_HEAD_DIM).astype(f32)
    bm = bm.reshape(b, s, SSM_GROUPS, SSM_STATE).astype(f32)
    cm = cm.reshape(b, s, SSM_GROUPS, SSM_STATE).astype(f32)
    dt_raw = dt_raw.astype(f32)
    dt_f = jax.nn.softplus(dt_raw[..., :SSM_HEADS] + dt_bias_f.astype(f32)).reshape(b, s, SSM_GROUPS, SSM_HPG)
    dt_b = jax.nn.softplus(dt_raw[..., SSM_HEADS:] + dt_bias_b.astype(f32)).reshape(b, s, SSM_GROUPS, SSM_HPG)
    a_f = -jnp.exp(a_log_f.astype(f32)).reshape(SSM_GROUPS, SSM_HPG)
    a_b = -jnp.exp(a_log_b.astype(f32)).reshape(SSM_GROUPS, SSM_HPG)
    flip = lambda t: jnp.flip(t, axis=1)
    y_f = ssd_chunked(xs, dt_f, a_f, bm, cm)
    y_b = flip(ssd_chunked(flip(xs), flip(dt_b), a_b, flip(bm), flip(cm)))
    y = y_f + y_b + xs * d_skip.astype(f32).reshape(SSM_GROUPS, SSM_HPG)[..., None]
    y = y.reshape(b, s, SSM_D) * jax.nn.silu(z.astype(f32))
    yg = y.reshape(b, s, SSM_GROUPS, SSM_D // SSM_GROUPS)
    yg = yg * lax.rsqrt(jnp.mean(yg * yg, axis=-1, keepdims=True) + EPS)
    y_ssd = (yg.reshape(b, s, SSM_D) * ssm_norm_w.astype(f32)).astype(hn.dtype)

    cos, sin = rope_tables(s)
    q = (rmsnorm(c_q, q_norm_w) @ w_uq).reshape(b, s, MLA_HEADS, QK_NOPE + QK_ROPE)
    q_nope, q_rope = q[..., :QK_NOPE], q[..., QK_NOPE:]
    q_rope = apply_rope(q_rope, cos[None, :, None], sin[None, :, None])
    kv = (rmsnorm(c_kv, kv_norm_w) @ w_ukv).reshape(b, s, MLA_HEADS, QK_NOPE + V_DIM)
    k_nope, v = kv[..., :QK_NOPE], kv[..., QK_NOPE:]
    k_r = apply_rope(k_r, cos[None], sin[None])
    q_full = jnp.concatenate([q_nope, q_rope], axis=-1)
    k_full = jnp.concatenate(
        [k_nope, jnp.broadcast_to(k_r[:, :, None], (b, s, MLA_HEADS, QK_ROPE))], axis=-1)
    o = block_attention(q_full, k_full, v, (QK_NOPE + QK_ROPE) ** -0.5)

    mixed = jnp.concatenate([y_ssd, o.reshape(b, s, MLA_D).astype(hn.dtype)], axis=-1)
    return mixed @ w_out


def fourier_mixer(hn, w_mix):
    b, s, _ = hn.shape
    xg = hn.astype(jnp.float32).reshape(b, s, FOURIER_GROUPS, FOURIER_GROUP_DIM)
    f = jnp.fft.fftn(xg, axes=(1, 3), norm='ortho').real
    return f.reshape(b, s, D_MODEL).astype(hn.dtype) @ w_mix


def cross_attention(hn, mem_n, wq, wkv, wo):
    b, s, _ = hn.shape
    m = mem_n.shape[1]
    q = (hn @ wq).reshape(b, s, XA_HEADS, XA_HEAD_DIM)
    k, v = jnp.split(mem_n @ wkv, 2, axis=-1)
    k = k.reshape(b, m, XA_HEADS, XA_HEAD_DIM)
    v = v.reshape(b, m, XA_HEADS, XA_HEAD_DIM)
    sc = jnp.einsum('bshd,bmhd->bhsm', q, k).astype(jnp.float32) * (XA_HEAD_DIM ** -0.5)
    p = jax.nn.softmax(sc, axis=-1).astype(v.dtype)
    o = jnp.einsum('bhsm,bmhd->bshd', p, v).reshape(b, s, D_MODEL)
    return o @ wo


def swiglu(hn, w_gu, w_down):
    g, u = jnp.split(hn @ w_gu, 2, axis=-1)
    return (jax.nn.silu(g) * u) @ w_down


def trunk(x, mem, layer_p, even_p, od_w_mix):
    (norm_mix_pre, norm_mix_post, norm_xa_pre, norm_xa_post, norm_mem, xa_wq, xa_wkv, xa_wo,
     norm_ffn_pre, norm_ffn_post, ffn_w_gu, ffn_w_down) = layer_p
    h = x
    for i in range(DEPTH):
        hn = rmsnorm(h, norm_mix_pre[i])
        if i % 2 == 0:
            mix = even_mixer(hn, *[p[i // 2] for p in even_p])
        else:
            mix = fourier_mixer(hn, od_w_mix[i // 2])
        h = h + rmsnorm(mix, norm_mix_post[i])
        mem_n = rmsnorm(mem, norm_mem[i])
        xa = cross_attention(rmsnorm(h, norm_xa_pre[i]), mem_n, xa_wq[i], xa_wkv[i], xa_wo[i])
        h = h + rmsnorm(xa, norm_xa_post[i])
        ff = swiglu(rmsnorm(h, norm_ffn_pre[i]), ffn_w_gu[i], ffn_w_down[i])
        h = h + rmsnorm(ff, norm_ffn_post[i])
    return h


def setup_inputs(seed: int = 0) -> dict:
    key = jax.random.key(seed)
    ks = iter(jax.random.split(key, 64))
    f32 = jnp.float32
    n_even = (DEPTH + 1) // 2
    n_odd = DEPTH // 2
    d = D_MODEL

    def nrm(shape, scale):
        return scale * jax.random.normal(next(ks), shape, f32)

    def gain(shape):
        return 1.0 + 0.05 * jax.random.normal(next(ks), shape, f32)

    def dt_bias(shape):
        u = jax.random.uniform(next(ks), shape, f32)
        dt = jnp.exp(u * (math.log(DT_MAX) - math.log(DT_MIN)) + math.log(DT_MIN))
        return dt + jnp.log(-jnp.expm1(-dt))

    def a_log(shape):
        return jnp.log(jax.random.uniform(next(ks), shape, f32, 1.0, 16.0))

    return {
        'x_prompt': nrm((BATCH, SEQ, d), 1.0),
        'x_sample': nrm((DEC_BATCH, DEC_SEQ, d), 1.0),
        'mem_prompt': nrm((BATCH, N_MEM, d), 1.0),
        'mem_sample': nrm((DEC_BATCH, N_MEM, d), 1.0),
        'norm_mix_pre': gain((DEPTH, d)),
        'norm_mix_post': gain((DEPTH, d)),
        'norm_xa_pre': gain((DEPTH, d)),
        'norm_xa_post': gain((DEPTH, d)),
        'norm_mem': gain((DEPTH, d)),
        'xa_wq': nrm((DEPTH, d, d), d ** -0.5),
        'xa_wkv': nrm((DEPTH, d, 2 * d), d ** -0.5),
        'xa_wo': nrm((DEPTH, d, d), d ** -0.5),
        'norm_ffn_pre': gain((DEPTH, d)),
        'norm_ffn_post': gain((DEPTH, d)),
        'ffn_w_gu': nrm((DEPTH, d, 2 * D_FF), d ** -0.5),
        'ffn_w_down': nrm((DEPTH, D_FF, d), D_FF ** -0.5),
        'ev_w_in': nrm((n_even, d, D_IN_EVEN), d ** -0.5),
        'ev_conv_w': nrm((n_even, D_CONV, CONV_CH), D_CONV ** -0.5),
        'ev_conv_b': nrm((n_even, CONV_CH), 0.02),
        'ev_a_log_f': a_log((n_even, SSM_HEADS)),
        'ev_a_log_b': a_log((n_even, SSM_HEADS)),
        'ev_dt_bias_f': dt_bias((n_even, SSM_HEADS)),
        'ev_dt_bias_b': dt_bias((n_even, SSM_HEADS)),
        'ev_d_skip': gain((n_even, SSM_HEADS)),
        'ev_ssm_norm': gain((n_even, SSM_D)),
        'ev_q_norm': gain((n_even, Q_LORA)),
        'ev_w_uq': nrm((n_even, Q_LORA, MLA_HEADS * (QK_NOPE + QK_ROPE)), Q_LORA ** -0.5),
        'ev_kv_norm': gain((n_even, KV_LORA)),
        'ev_w_ukv': nrm((n_even, KV_LORA, MLA_HEADS * (QK_NOPE + V_DIM)), KV_LORA ** -0.5),
        'ev_w_out': nrm((n_even, D_MIX_EVEN, d), D_MIX_EVEN ** -0.5),
        'od_w_mix': nrm((n_odd, d, d), d ** -0.5),
    }


def reference(x_prompt, x_sample, mem_prompt, mem_sample,
              norm_mix_pre, norm_mix_post, norm_xa_pre, norm_xa_post, norm_mem,
              xa_wq, xa_wkv, xa_wo, norm_ffn_pre, norm_ffn_post, ffn_w_gu, ffn_w_down,
              ev_w_in, ev_conv_w, ev_conv_b, ev_a_log_f, ev_a_log_b, ev_dt_bias_f, ev_dt_bias_b,
              ev_d_skip, ev_ssm_norm, ev_q_norm, ev_w_uq, ev_kv_norm, ev_w_ukv, ev_w_out,
              od_w_mix):
    layer_p = (norm_mix_pre, norm_mix_post, norm_xa_pre, norm_xa_post, norm_mem, xa_wq, xa_wkv, xa_wo,
               norm_ffn_pre, norm_ffn_post, ffn_w_gu, ffn_w_down)
    even_p = (ev_w_in, ev_conv_w, ev_conv_b, ev_a_log_f, ev_a_log_b, ev_dt_bias_f, ev_dt_bias_b,
              ev_d_skip, ev_ssm_norm, ev_q_norm, ev_w_uq, ev_kv_norm, ev_w_ukv, ev_w_out)
    y_prompt = trunk(x_prompt, mem_prompt, layer_p, even_p, od_w_mix)
    y_sample = trunk(x_sample, mem_sample, layer_p, even_p, od_w_mix)
    return (y_prompt, y_sample)
```

```python
import functools
import math

import jax
import jax.numpy as jnp
from jax import lax
from jax.experimental import pallas as pl
from jax.experimental.pallas import tpu as pltpu

F32 = jnp.float32
BF16 = jnp.bfloat16

EPS = 1e-6
D_MODEL = 1024
N_MEM = 256

SSM_HEADS = 8
SSM_HEAD_DIM = 64
SSM_D = SSM_HEADS * SSM_HEAD_DIM
SSM_GROUPS = 2
SSM_HPG = SSM_HEADS // SSM_GROUPS
SSM_STATE = 128
D_CONV = 5
CONV_CH = SSM_D + 2 * SSM_GROUPS * SSM_STATE
CHUNK = 128

MLA_HEADS = 8
QK_NOPE = 64
QK_ROPE = 32
V_DIM = 64
Q_LORA = 256
KV_LORA = 128
ROPE_THETA = 10000.0
MLA_D = MLA_HEADS * V_DIM

FOURIER_GROUPS = 4
FOURIER_GROUP_DIM = D_MODEL // FOURIER_GROUPS

XA_HEADS = 4
XA_HEAD_DIM = D_MODEL // XA_HEADS

LANE = 128
SUBLANE = 8
HEAD_PAD = LANE
V_ONE_LANE = V_DIM
VMEM_LIMIT = 56 * 1024 * 1024

TOKEN_TILE = 512
FFN_CHUNK = 256
ATTN_TQ = 1024
ATTN_TK = 512
LOG2E = 1.4426950408889634

O_Z = 0
O_XBC = O_Z + SSM_D
O_CQ = O_XBC + CONV_CH
O_CKV = O_CQ + Q_LORA
O_KA = O_CKV + KV_LORA
O_KB = O_KA + LANE
O_DT = O_KB + LANE
D_IN_PAD = O_DT + LANE


def _params(sem, vmem=VMEM_LIMIT):
    return pltpu.CompilerParams(dimension_semantics=sem, vmem_limit_bytes=vmem)


def _rms(x, w):
    ms = jnp.mean(x * x, axis=-1, keepdims=True)
    return x * lax.rsqrt(ms + EPS) * w


def _silu(x):
    return x / (1.0 + jnp.exp(-x))


def _dot(a, b):
    return jnp.dot(a, b, preferred_element_type=F32)


def _dot_nt(a, b):
    return lax.dot_general(a, b, (((1,), (1,)), ((), ())), preferred_element_type=F32)


def _const_spec(shape):
    nd = len(shape)
    return pl.BlockSpec(shape, lambda *_: (0,) * nd)


def _memkv_kernel(mem_ref, nw_ref, wkv_ref, kv_ref):
    xn = _rms(mem_ref[0], nw_ref[...]).astype(BF16)
    kv_ref[0] = _dot(xn, wkv_ref[...]).astype(BF16)


def _memkv(mem, nw, wkv):
    b, m, d = mem.shape
    return pl.pallas_call(
        _memkv_kernel,
        grid=(b,),
        in_specs=[pl.BlockSpec((1, m, d), lambda i: (i, 0, 0)),
                  _const_spec((1, d)),
                  _const_spec((d, 2 * d))],
        out_specs=pl.BlockSpec((1, m, 2 * d), lambda i: (i, 0, 0)),
        out_shape=jax.ShapeDtypeStruct((b, m, 2 * d), BF16),
        compiler_params=_params(("parallel",)),
        name="memkv",
    )(mem, nw, wkv)


def _inproj_kernel(x_ref, nw_ref, win_ref, qnw_ref, kvnw_ref, wqm_ref, wqr_ref, wk_ref, wv_ref,
                   cos_ref, sin_ref, z_ref, xbc_ref, dt_ref, q_ref, k_ref, v_ref):
    hn = _rms(x_ref[...], nw_ref[...]).astype(BF16)
    proj = _dot(hn, win_ref[...])
    z_ref[...] = proj[:, O_Z:O_XBC]
    xbc_ref[...] = proj[:, O_XBC:O_CQ]
    dt_ref[...] = proj[:, O_DT:D_IN_PAD]
    cqn = _rms(proj[:, O_CQ:O_CKV], qnw_ref[...]).astype(BF16)
    ckvn = _rms(proj[:, O_CKV:O_KA], kvnw_ref[...]).astype(BF16)
    cos_t = cos_ref[...]
    sin_t = sin_ref[...]
    qscale = LOG2E * (QK_NOPE + QK_ROPE) ** -0.5
    cos_q = cos_t * qscale
    sin_q = sin_t * qscale
    qm = _dot(cqn, wqm_ref[...])
    qr = _dot(cqn, wqr_ref[...])
    kr = proj[:, O_KA:O_KB] * cos_t + proj[:, O_KB:O_DT] * sin_t
    km = _dot(ckvn, wk_ref[...])
    vm = _dot(ckvn, wv_ref[...])
    lane = lax.broadcasted_iota(jnp.int32, (1, HEAD_PAD), 1)
    one_col = jnp.where(lane == V_ONE_LANE, 1.0, 0.0).astype(F32)
    for h in range(MLA_HEADS):
        sl = slice(h * HEAD_PAD, (h + 1) * HEAD_PAD)
        q_ref[:, sl] = (qm[:, sl] * cos_q + qr[:, sl] * sin_q).astype(BF16)
        k_ref[:, sl] = (km[:, sl] + kr).astype(BF16)
        v_ref[:, sl] = (vm[:, sl] + one_col).astype(BF16)


def _inproj(x2, nw, win, qnw, kvnw, wqm, wqr, wk, wv, cos_t, sin_t, seq):
    n, d = x2.shape
    t = TOKEN_TILE
    tiles_per_seq = seq // t
    tok = lambda w: pl.BlockSpec((t, w), lambda i: (i, 0))
    pos = pl.BlockSpec((t, HEAD_PAD), lambda i: (i % tiles_per_seq, 0))
    hp = MLA_HEADS * HEAD_PAD
    outs = [jax.ShapeDtypeStruct((n, SSM_D), F32), jax.ShapeDtypeStruct((n, CONV_CH), F32),
            jax.ShapeDtypeStruct((n, LANE), F32), jax.ShapeDtypeStruct((n, hp), BF16),
            jax.ShapeDtypeStruct((n, hp), BF16), jax.ShapeDtypeStruct((n, hp), BF16)]
    return pl.pallas_call(
        _inproj_kernel,
        grid=(n // t,),
        in_specs=[tok(d), _const_spec((1, d)), _const_spec((d, D_IN_PAD)),
                  _const_spec((1, Q_LORA)), _const_spec((1, KV_LORA)),
                  _const_spec((Q_LORA, hp)), _const_spec((Q_LORA, hp)),
                  _const_spec((KV_LORA, hp)), _const_spec((KV_LORA, hp)),
                  pos, pos],
        out_specs=[tok(SSM_D), tok(CONV_CH), tok(LANE), tok(hp), tok(hp), tok(hp)],
        out_shape=outs,
        compiler_params=_params(("parallel",)),
        name="inproj",
    )(x2, nw, win, qnw, kvnw, wqm, wqr, wk, wv, cos_t, sin_t)


def _conv_silu(prev_ref, cur_ref, next_ref, has_prev, has_next, cw_ref, cb_ref):
    prev = jnp.where(has_prev, prev_ref[0], 0.0)
    nxt = jnp.where(has_next, next_ref[0], 0.0)
    ext = jnp.concatenate([prev, cur_ref[0], nxt], axis=0)
    acc = cb_ref[...] + jnp.zeros((CHUNK, CONV_CH), F32)
    base = SUBLANE - D_CONV // 2
    for j in range(D_CONV):
        acc = acc + ext[base + j:base + j + CHUNK, :] * cw_ref[j:j + 1, :]
    return _silu(acc)


def _softplus(x):
    return jnp.maximum(x, 0.0) + jnp.log1p(jnp.exp(-jnp.abs(x)))


def _ssd_chunk(xbc, dt_raw, dtb_ref, a_ref, h_ref, reverse, lane_off):
    L = CHUNK
    xs = xbc[:, :SSM_D]
    bm = xbc[:, SSM_D:SSM_D + SSM_GROUPS * SSM_STATE]
    cm = xbc[:, SSM_D + SSM_GROUPS * SSM_STATE:]
    dt = _softplus(dt_raw + dtb_ref[...])
    dta = dt * a_ref[...]
    row = lax.broadcasted_iota(jnp.int32, (L, L), 0)
    col = lax.broadcasted_iota(jnp.int32, (L, L), 1)
    mask = (col >= row) if reverse else (col <= row)
    tri = jnp.where(mask, 1.0, 0.0).astype(F32)
    cum = jnp.dot(tri, dta, preferred_element_type=F32, precision=lax.Precision.HIGHEST)
    cum_t = cum.T
    dt_t = dt.T
    total = cum[0:1, :] if reverse else cum[L - 1:L, :]
    w_state = dt * jnp.exp(total - cum)
    exp_cum = jnp.exp(cum)
    exp_total = jnp.exp(total)
    ys = []
    for g in range(SSM_GROUPS):
        bg = bm[:, g * SSM_STATE:(g + 1) * SSM_STATE]
        cg = cm[:, g * SSM_STATE:(g + 1) * SSM_STATE].astype(BF16)
        cb = _dot_nt(cg, bg.astype(BF16))
        h_prev = h_ref[g]
        y_off = _dot(cg, h_prev.astype(BF16))
        xw_parts = []
        dec_parts = []
        for r in range(SSM_HPG):
            hh = g * SSM_HPG + r
            ln = lane_off + hh
            x_h = xs[:, hh * SSM_HEAD_DIM:(hh + 1) * SSM_HEAD_DIM]
            seg = cum[:, ln:ln + 1] - cum_t[ln:ln + 1, :]
            dec = jnp.exp(jnp.where(mask, seg, -jnp.inf))
            m = (cb * dec * dt_t[ln:ln + 1, :]).astype(BF16)
            y_h = _dot(m, x_h.astype(BF16)) + \
                y_off[:, r * SSM_HEAD_DIM:(r + 1) * SSM_HEAD_DIM] * exp_cum[:, ln:ln + 1]
            ys.append(y_h)
            xw_parts.append(x_h * w_state[:, ln:ln + 1])
            dec_parts.append(jnp.broadcast_to(exp_total[:, ln:ln + 1], (1, SSM_HEAD_DIM)))
        xw = jnp.concatenate(xw_parts, axis=1).astype(BF16)
        st = _dot(bg.T.astype(BF16), xw)
        h_ref[g] = h_prev * jnp.concatenate(dec_parts, axis=1) + st
    return jnp.concatenate(ys, axis=1), xs


def _ssd_kernel(fp_ref, fc_ref, fn_ref, bp_ref, bc_ref, bn_ref, dtf_ref, dtb_ref,
                cw_ref, cb_ref, bias_ref, a_ref, dskip_ref, yf_ref, yb_ref, hf_ref, hb_ref):
    c = pl.program_id(1)
    nc = pl.num_programs(1)

    @pl.when(c == 0)
    def _():
        hf_ref[...] = jnp.zeros_like(hf_ref)
        hb_ref[...] = jnp.zeros_like(hb_ref)

    xbc_f = _conv_silu(fp_ref, fc_ref, fn_ref, c > 0, c < nc - 1, cw_ref, cb_ref)
    y_f, xs_f = _ssd_chunk(xbc_f, dtf_ref[0], bias_ref, a_ref, hf_ref, False, 0)
    yf_ref[0] = y_f + xs_f * dskip_ref[...]
    xbc_b = _conv_silu(bp_ref, bc_ref, bn_ref, c < nc - 1, c > 0, cw_ref, cb_ref)
    y_b, _ = _ssd_chunk(xbc_b, dtb_ref[0], bias_ref, a_ref, hb_ref, True, SSM_HEADS)
    yb_ref[0] = y_b


def _ssd(xbc, dt, cw, cb, bias, a_row, dskip):
    b, s, _ = xbc.shape
    nc = s // CHUNK
    rb = CHUNK // SUBLANE
    nrb = s // SUBLANE
    cur = lambda f: pl.BlockSpec((1, CHUNK, CONV_CH), lambda i, c: (i, f(c, nc), 0))
    prv = lambda f: pl.BlockSpec((1, SUBLANE, CONV_CH),
                                 lambda i, c: (i, jnp.maximum(f(c, nc) * rb - 1, 0), 0))
    nxt = lambda f: pl.BlockSpec((1, SUBLANE, CONV_CH),
                                 lambda i, c: (i, jnp.minimum(f(c, nc) * rb + rb, nrb - 1), 0))
    dts = lambda f: pl.BlockSpec((1, CHUNK, LANE), lambda i, c: (i, f(c, nc), 0))
    fwd = lambda c, n: c
    bwd = lambda c, n: n - 1 - c
    ysp = lambda f: pl.BlockSpec((1, CHUNK, SSM_D), lambda i, c: (i, f(c, nc), 0))
    hshape = (SSM_GROUPS, SSM_STATE, SSM_HPG * SSM_HEAD_DIM)
    return pl.pallas_call(
        _ssd_kernel,
        grid=(b, nc),
        in_specs=[prv(fwd), cur(fwd), nxt(fwd), prv(bwd), cur(bwd), nxt(bwd), dts(fwd), dts(bwd),
                  _const_spec((SUBLANE, CONV_CH)), _const_spec((1, CONV_CH)),
                  _const_spec((1, LANE)), _const_spec((1, LANE)), _const_spec((1, SSM_D))],
        out_specs=[ysp(fwd), ysp(bwd)],
        out_shape=[jax.ShapeDtypeStruct((b, s, SSM_D), F32)] * 2,
        scratch_shapes=[pltpu.VMEM(hshape, F32), pltpu.VMEM(hshape, F32)],
        compiler_params=_params(("parallel", "arbitrary")),
        name="ssd",
    )(xbc, xbc, xbc, xbc, xbc, xbc, dt, dt, cw, cb, bias, a_row, dskip)


def _flash_kernel(q_ref, k_ref, v_ref, o_ref, *, tk):
    s_len = k_ref.shape[1]
    tq = q_ref.shape[1]
    outs = []
    for hh in range(2):
        sl = slice(hh * HEAD_PAD, (hh + 1) * HEAD_PAD)
        qh = q_ref[0, :, sl]

        def body(j, carry, sl=sl, qh=qh):
            m, acc = carry
            off = pl.multiple_of(j * tk, tk)
            kj = k_ref[0, pl.ds(off, tk), sl]
            vj = v_ref[0, pl.ds(off, tk), sl]
            s = _dot_nt(qh, kj)
            m_new = jnp.maximum(m, jnp.max(s, axis=1, keepdims=True))
            alpha = jnp.exp2(m - m_new)
            p = jnp.exp2(s - m_new).astype(BF16)
            return m_new, acc * alpha + _dot(p, vj)

        m0 = jnp.full((tq, 1), -jnp.inf, F32)
        acc0 = jnp.zeros((tq, HEAD_PAD), F32)
        _, acc = lax.fori_loop(0, s_len // tk, body, (m0, acc0))
        outs.append(acc / acc[:, V_ONE_LANE:V_ONE_LANE + 1])
    lane = lax.broadcasted_iota(jnp.int32, (tq, HEAD_PAD), 1)
    o_ref[0] = jnp.where(lane < V_DIM, outs[0], pltpu.roll(outs[1], V_DIM, 1)).astype(BF16)


def _flash(q, k, v):
    b, s, hp = q.shape
    tq = min(ATTN_TQ, s)
    tk = min(ATTN_TK, s)
    pairs = MLA_HEADS // 2
    pw = 2 * HEAD_PAD
    return pl.pallas_call(
        functools.partial(_flash_kernel, tk=tk),
        grid=(b, pairs, s // tq),
        in_specs=[pl.BlockSpec((1, tq, pw), lambda i, p, j: (i, j, p)),
                  pl.BlockSpec((1, s, pw), lambda i, p, j: (i, 0, p)),
                  pl.BlockSpec((1, s, pw), lambda i, p, j: (i, 0, p))],
        out_specs=pl.BlockSpec((1, tq, 2 * V_DIM), lambda i, p, j: (i, j, p)),
        out_shape=jax.ShapeDtypeStruct((b, s, MLA_D), BF16),
        compiler_params=_params(("parallel", "parallel", "arbitrary")),
        name="mla_flash",
    )(q, k, v)


def _cross_attn(h1, kv_ref, pre_w, wq_ref, wo_ref, post_w):
    hn = _rms(h1, pre_w).astype(BF16)
    q = (_dot(hn, wq_ref[...]) * (XA_HEAD_DIM ** -0.5)).astype(BF16)
    heads = []
    for hd in range(XA_HEADS):
        sl = slice(hd * XA_HEAD_DIM, (hd + 1) * XA_HEAD_DIM)
        kh = kv_ref[0, :, sl]
        vh = kv_ref[0, :, D_MODEL + hd * XA_HEAD_DIM:D_MODEL + (hd + 1) * XA_HEAD_DIM]
        s = _dot_nt(q[:, sl], kh)
        p = jnp.exp(s - jnp.max(s, axis=1, keepdims=True))
        l = jnp.sum(p, axis=1, keepdims=True)
        heads.append((_dot(p.astype(BF16), vh) / l).astype(BF16))
    o = jnp.concatenate(heads, axis=1)
    xa = _dot(o, wo_ref[...])
    return h1 + _rms(xa, post_w)


def _postmix_even_kernel(h_ref, yf_ref, yb_ref, z_ref, o_ref, snw_ref, wout_ref, mpost_ref,
                         kv_ref, xpre_ref, wq_ref, wo_ref, xpost_ref, out_ref):
    y = (yf_ref[0] + yb_ref[0]) * _silu(z_ref[0])
    gw = SSM_D // SSM_GROUPS
    parts = []
    for g in range(SSM_GROUPS):
        yg = y[:, g * gw:(g + 1) * gw]
        parts.append(yg * lax.rsqrt(jnp.mean(yg * yg, axis=-1, keepdims=True) + EPS))
    y_ssd = (jnp.concatenate(parts, axis=1) * snw_ref[...]).astype(BF16)
    mix = _dot(y_ssd, wout_ref[:SSM_D, :]) + _dot(o_ref[0], wout_ref[SSM_D:, :])
    h1 = h_ref[0] + _rms(mix, mpost_ref[...])
    out_ref[0] = _cross_attn(h1, kv_ref, xpre_ref[...], wq_ref, wo_ref, xpost_ref[...])


def _postmix_odd_kernel(h_ref, f_ref, wmix_ref, mpost_ref,
                        kv_ref, xpre_ref, wq_ref, wo_ref, xpost_ref, out_ref):
    mix = _dot(f_ref[0], wmix_ref[...])
    h1 = h_ref[0] + _rms(mix, mpost_ref[...])
    out_ref[0] = _cross_attn(h1, kv_ref, xpre_ref[...], wq_ref, wo_ref, xpost_ref[...])


def _postmix(kernel, h, mixed, mixed_w, consts_a, kv, kv_off, consts_b):
    b, s, d = h.shape
    t = TOKEN_TILE
    tok = lambda w: pl.BlockSpec((1, t, w), lambda i, j: (i, j, 0))
    in_specs = [tok(d)] + [tok(w) for w in mixed_w]
    in_specs += [_const_spec(c.shape) for c in consts_a]
    in_specs += [pl.BlockSpec((1, N_MEM, 2 * d), lambda i, j: (i + kv_off, 0, 0))]
    in_specs += [_const_spec(c.shape) for c in consts_b]
    return pl.pallas_call(
        kernel,
        grid=(b, s // t),
        in_specs=in_specs,
        out_specs=tok(d),
        out_shape=jax.ShapeDtypeStruct((b, s, d), F32),
        compiler_params=_params(("parallel", "parallel")),
        name="postmix",
    )(h, *mixed, *consts_a, kv, *consts_b)


def _ffn_kernel(h_ref, pre_ref, wg_ref, wu_ref, wd_ref, post_ref, out_ref):
    h = h_ref[...]
    hn = _rms(h, pre_ref[...]).astype(BF16)
    d_ff = wg_ref.shape[1]
    acc = jnp.zeros(h.shape, F32)
    for c in range(d_ff // FFN_CHUNK):
        sl = slice(c * FFN_CHUNK, (c + 1) * FFN_CHUNK)
        g = _dot(hn, wg_ref[:, sl])
        u = _dot(hn, wu_ref[:, sl])
        acc = acc + _dot((_silu(g) * u).astype(BF16), wd_ref[sl, :])
    out_ref[...] = h + _rms(acc, post_ref[...])


def _ffn(h2, pre, wg, wu, wd, post):
    n, d = h2.shape
    t = TOKEN_TILE
    tok = pl.BlockSpec((t, d), lambda i: (i, 0))
    return pl.pallas_call(
        _ffn_kernel,
        grid=(n // t,),
        in_specs=[tok, _const_spec((1, d)), _const_spec(wg.shape), _const_spec(wu.shape),
                  _const_spec(wd.shape), _const_spec((1, d))],
        out_specs=tok,
        out_shape=jax.ShapeDtypeStruct((n, d), F32),
        compiler_params=_params(("parallel",)),
        name="ffn",
    )(h2, pre, wg, wu, wd, post)


def _fnet_a_kernel(x_ref, nw_ref, cs_ref, m1_ref, a_ref):
    xn = _rms(x_ref[0], nw_ref[...]).astype(BF16)
    gd = FOURIER_GROUP_DIM
    yr, yi = [], []
    for g in range(FOURIER_GROUPS):
        y = _dot(xn[:, g * gd:(g + 1) * gd], cs_ref[...])
        yr.append(y[:, :gd])
        yi.append(y[:, gd:])
    stack = jnp.concatenate([jnp.concatenate(yr, axis=1), jnp.concatenate(yi, axis=1)], axis=0)
    a_ref[0] = _dot(m1_ref[...], stack.astype(BF16)).astype(BF16)


def _fnet_b_kernel(a_ref, g_ref, f_ref):
    stack = jnp.concatenate([a_ref[0, 0, 0], a_ref[0, 1, 0]], axis=0)
    f_ref[0] = _dot(g_ref[0], stack).astype(BF16)


def _fnet(h, nw, cs, m1, gt):
    b, s, d = h.shape
    n1 = m1.shape[0] // 2
    n2 = s // n1
    a = pl.pallas_call(
        _fnet_a_kernel,
        grid=(b, n2),
        in_specs=[pl.BlockSpec((1, n1, d), lambda i, j: (i, 0, j)),
                  _const_spec((1, d)), _const_spec(cs.shape), _const_spec(m1.shape)],
        out_specs=pl.BlockSpec((1, 2 * n1, d), lambda i, j: (i, 0, j)),
        out_shape=jax.ShapeDtypeStruct((b, 2 * n1, n2 * d), BF16),
        compiler_params=_params(("parallel", "parallel")),
        name="fnet_a",
    )(h.reshape(b, n1, n2 * d), nw, cs, m1)
    f = pl.pallas_call(
        _fnet_b_kernel,
        grid=(b, n1),
        in_specs=[pl.BlockSpec((1, 2, 1, n2, d), lambda i, j: (i, 0, j, 0, 0)),
                  pl.BlockSpec((1, n2, 2 * n2), lambda i, j: (j, 0, 0))],
        out_specs=pl.BlockSpec((1, n2, d), lambda i, j: (i, 0, j)),
        out_shape=jax.ShapeDtypeStruct((b, n2, n1 * d), BF16),
        compiler_params=_params(("parallel", "parallel")),
        name="fnet_b",
    )(a.reshape(b, 2, n1, n2, d), gt)
    return f.reshape(b, s, d)


def _rope_tables(s):
    inv = ROPE_THETA ** (-jnp.arange(0, QK_ROPE, 2, dtype=F32) / QK_ROPE)
    ang = jnp.arange(s, dtype=F32)[:, None] * inv[None, :]
    cos2 = jnp.concatenate([jnp.cos(ang), jnp.cos(ang)], axis=1)
    sin2 = jnp.concatenate([jnp.sin(ang), jnp.sin(ang)], axis=1)
    pad = HEAD_PAD - QK_NOPE - QK_ROPE
    cos_t = jnp.concatenate([jnp.ones((s, QK_NOPE), F32), cos2, jnp.ones((s, pad), F32)], axis=1)
    sin_t = jnp.concatenate([jnp.zeros((s, QK_NOPE), F32), sin2, jnp.zeros((s, pad), F32)], axis=1)
    return cos_t, sin_t


def _rot_cols(w):
    half = w.shape[-1] // 2
    return jnp.concatenate([-w[..., half:], w[..., :half]], axis=-1)


def _pad_cols(w, left, total):
    return jnp.pad(w, ((0, 0), (left, total - left - w.shape[1])))


def _even_weights(w_in, w_uq, w_ukv):
    o1 = SSM_D
    o2 = o1 + CONV_CH
    o3 = o2 + 2 * SSM_HEADS
    o4 = o3 + Q_LORA
    o5 = o4 + KV_LORA
    w_z, w_xbc, w_dt, w_cq, w_ckv, w_kr = (w_in[:, :o1], w_in[:, o1:o2], w_in[:, o2:o3],
                                             w_in[:, o3:o4], w_in[:, o4:o5], w_in[:, o5:])
    win = jnp.concatenate([
        w_z, w_xbc, w_cq, w_ckv,
        _pad_cols(w_kr, QK_NOPE, LANE), _pad_cols(_rot_cols(w_kr), QK_NOPE, LANE),
        _pad_cols(w_dt, 0, LANE)], axis=1).astype(BF16)
    dq = QK_NOPE + QK_ROPE
    wq = w_uq.reshape(Q_LORA, MLA_HEADS, dq)
    zq = jnp.zeros((Q_LORA, MLA_HEADS, HEAD_PAD - dq), F32)
    wqm = jnp.concatenate([wq, zq], axis=-1).reshape(Q_LORA, -1).astype(BF16)
    wqr = jnp.concatenate([jnp.zeros((Q_LORA, MLA_HEADS, QK_NOPE), F32),
                           _rot_cols(wq[..., QK_NOPE:]), zq], axis=-1).reshape(Q_LORA, -1).astype(BF16)
    wkv = w_ukv.reshape(KV_LORA, MLA_HEADS, QK_NOPE + V_DIM)
    zk = jnp.zeros((KV_LORA, MLA_HEADS, HEAD_PAD - QK_NOPE), F32)
    wk = jnp.concatenate([wkv[..., :QK_NOPE], zk], axis=-1).reshape(KV_LORA, -1).astype(BF16)
    zv = jnp.zeros((KV_LORA, MLA_HEADS, HEAD_PAD - V_DIM), F32)
    wv = jnp.concatenate([wkv[..., QK_NOPE:], zv], axis=-1).reshape(KV_LORA, -1).astype(BF16)
    return win, wqm, wqr, wk, wv


def _fnet_tables(s):
    n2 = CHUNK
    n1 = s // n2
    gd = FOURIER_GROUP_DIM
    ci = jnp.arange(gd, dtype=jnp.int32)
    ang_c = (2.0 * math.pi / gd) * ((ci[:, None] * ci[None, :]) % gd).astype(F32)
    cs = (jnp.concatenate([jnp.cos(ang_c), -jnp.sin(ang_c)], axis=1) * gd ** -0.5).astype(BF16)
    i1 = jnp.arange(n1, dtype=jnp.int32)
    ang1 = (2.0 * math.pi / n1) * ((i1[:, None] * i1[None, :]) % n1).astype(F32)
    c1, s1 = jnp.cos(ang1), jnp.sin(ang1)
    m1 = jnp.concatenate([jnp.concatenate([c1, s1], axis=1),
                          jnp.concatenate([-s1, c1], axis=1)], axis=0).astype(BF16)
    k1 = jnp.arange(n1, dtype=jnp.int32)[:, None, None]
    k2 = jnp.arange(n2, dtype=jnp.int32)[None, :, None]
    j2 = jnp.arange(n2, dtype=jnp.int32)[None, None, :]
    ang = (2.0 * math.pi / s) * ((j2 * (k1 + n1 * k2)) % s).astype(F32)
    gt = (jnp.concatenate([jnp.cos(ang), jnp.sin(ang)], axis=2) * s ** -0.5).astype(BF16)
    return cs, m1, gt


def _row(v, width=None):
    v = v.astype(F32).reshape(1, -1)
    if width is not None:
        v = jnp.pad(v, ((0, 0), (0, width - v.shape[1])))
    return v


def _trunk(x, kv_layers, kv_off, p):
    b, s, d = x.shape
    n = b * s
    h = x
    z, xbc, dt, q, k, v = _inproj(h.reshape(n, d), p["mix_pre"][0], p["win"], p["q_norm"], p["kv_norm"],
                                  p["wqm"], p["wqr"], p["wk"], p["wv"], p["cos"][:s], p["sin"][:s], s)
    yf, yb = _ssd(xbc.reshape(b, s, -1), dt.reshape(b, s, -1), p["conv_w"], p["conv_b"],
                  p["dt_bias"], p["a_row"], p["d_skip"])
    hp = MLA_HEADS * HEAD_PAD
    o = _flash(q.reshape(b, s, hp), k.reshape(b, s, hp), v.reshape(b, s, hp))
    h = _postmix(_postmix_even_kernel, h, (yf, yb, z.reshape(b, s, -1), o),
                 (SSM_D, SSM_D, SSM_D, MLA_D),
                 (p["ssm_norm"], p["w_out"], p["mix_post"][0]), kv_layers[0], kv_off,
                 (p["xa_pre"][0], p["xa_wq"][0], p["xa_wo"][0], p["xa_post"][0]))
    h = _ffn(h.reshape(n, d), p["ffn_pre"][0], p["wg"][0], p["wu"][0], p["wd"][0],
             p["ffn_post"][0]).reshape(b, s, d)
    cs, m1, gt = _fnet_tables(s)
    f = _fnet(h, p["mix_pre"][1], cs, m1, gt)
    h = _postmix(_postmix_odd_kernel, h, (f,), (d,), (p["w_mix"], p["mix_post"][1]),
                 kv_layers[1], kv_off,
                 (p["xa_pre"][1], p["xa_wq"][1], p["xa_wo"][1], p["xa_post"][1]))
    h = _ffn(h.reshape(n, d), p["ffn_pre"][1], p["wg"][1], p["wu"][1], p["wd"][1],
             p["ffn_post"][1]).reshape(b, s, d)
    return h


def kernel(x_prompt, x_sample, mem_prompt, mem_sample, norm_mix_pre, norm_mix_post, norm_xa_pre, norm_xa_post, norm_mem, xa_wq, xa_wkv, xa_wo, norm_ffn_pre, norm_ffn_post, ffn_w_gu, ffn_w_down, ev_w_in, ev_conv_w, ev_conv_b, ev_a_log_f, ev_a_log_b, ev_dt_bias_f, ev_dt_bias_b, ev_d_skip, ev_ssm_norm, ev_q_norm, ev_w_uq, ev_kv_norm, ev_w_ukv, ev_w_out, od_w_mix):
    depth = norm_mix_pre.shape[0]
    d_ff = ffn_w_down.shape[1]
    s_max = max(x_prompt.shape[1], x_sample.shape[1])
    cos_t, sin_t = _rope_tables(s_max)
    win, wqm, wqr, wk, wv = _even_weights(ev_w_in[0], ev_w_uq[0], ev_w_ukv[0])
    rows = lambda w: [_row(w[i]) for i in range(depth)]
    p = {
        "mix_pre": rows(norm_mix_pre), "mix_post": rows(norm_mix_post),
        "xa_pre": rows(norm_xa_pre), "xa_post": rows(norm_xa_post),
        "ffn_pre": rows(norm_ffn_pre), "ffn_post": rows(norm_ffn_post),
        "xa_wq": [xa_wq[i].astype(BF16) for i in range(depth)],
        "xa_wo": [xa_wo[i].astype(BF16) for i in range(depth)],
        "wg": [ffn_w_gu[i, :, :d_ff].astype(BF16) for i in range(depth)],
        "wu": [ffn_w_gu[i, :, d_ff:].astype(BF16) for i in range(depth)],
        "wd": [ffn_w_down[i].astype(BF16) for i in range(depth)],
        "win": win, "wqm": wqm, "wqr": wqr, "wk": wk, "wv": wv,
        "q_norm": _row(ev_q_norm[0]), "kv_norm": _row(ev_kv_norm[0]),
        "cos": cos_t, "sin": sin_t,
        "conv_w": jnp.pad(ev_conv_w[0].astype(F32), ((0, SUBLANE - D_CONV), (0, 0))),
        "conv_b": _row(ev_conv_b[0]),
        "dt_bias": _row(jnp.concatenate([ev_dt_bias_f[0], ev_dt_bias_b[0]]), LANE),
        "a_row": _row(-jnp.exp(jnp.concatenate([ev_a_log_f[0], ev_a_log_b[0]]).astype(F32)), LANE),
        "d_skip": _row(jnp.repeat(ev_d_skip[0].astype(F32), SSM_HEAD_DIM)),
        "ssm_norm": _row(ev_ssm_norm[0]),
        "w_out": ev_w_out[0].astype(BF16),
        "w_mix": od_w_mix[0].astype(BF16),
    }
    mem = jnp.concatenate([mem_prompt, mem_sample], axis=0)
    kv_layers = [_memkv(mem, _row(norm_mem[i]), xa_wkv[i].astype(BF16)) for i in range(depth)]
    y_prompt = _trunk(x_prompt, kv_layers, 0, p)
    y_sample = _trunk(x_sample, kv_layers, x_prompt.shape[0], p)
    return (y_prompt, y_sample)
```

```python
import functools
import math

import jax
import jax.numpy as jnp
from jax import lax
from jax.experimental import pallas as pl
from jax.experimental.pallas import tpu as pltpu

F32 = jnp.float32
BF16 = jnp.bfloat16

EPS = 1e-6
D_MODEL = 1024
N_MEM = 256

SSM_HEADS = 8
SSM_HEAD_DIM = 64
SSM_D = SSM_HEADS * SSM_HEAD_DIM
SSM_GROUPS = 2
SSM_HPG = SSM_HEADS // SSM_GROUPS
SSM_STATE = 128
D_CONV = 5
CONV_CH = SSM_D + 2 * SSM_GROUPS * SSM_STATE
CHUNK = 128

MLA_HEADS = 8
QK_NOPE = 64
QK_ROPE = 32
V_DIM = 64
Q_LORA = 256
KV_LORA = 128
ROPE_THETA = 10000.0
MLA_D = MLA_HEADS * V_DIM

FOURIER_GROUPS = 4
FOURIER_GROUP_DIM = D_MODEL // FOURIER_GROUPS

XA_HEADS = 4
XA_HEAD_DIM = D_MODEL // XA_HEADS

LANE = 128
SUBLANE = 8
HEAD_PAD = LANE
V_ONE_LANE = V_DIM
VMEM_LIMIT = 56 * 1024 * 1024

TOKEN_TILE = 512
FFN_CHUNK = 256
ATTN_TQ = 1024
ATTN_TK = 512
ATTN_UNROLL = 4
FNET_COLS = 8
LOG2E = 1.4426950408889634

O_Z = 0
O_XBC = O_Z + SSM_D
O_CQ = O_XBC + CONV_CH
O_CKV = O_CQ + Q_LORA
O_KA = O_CKV + KV_LORA
O_KB = O_KA + LANE
O_DT = O_KB + LANE
D_IN_PAD = O_DT + LANE


def _params(sem, vmem=VMEM_LIMIT):
    return pltpu.CompilerParams(dimension_semantics=sem, vmem_limit_bytes=vmem)


def _rms(x, w):
    ms = jnp.mean(x * x, axis=-1, keepdims=True)
    return x * lax.rsqrt(ms + EPS) * w


def _silu(x):
    return x / (1.0 + jnp.exp(-x))


def _dot(a, b):
    return jnp.dot(a, b, preferred_element_type=F32)


def _dot_nt(a, b):
    return lax.dot_general(a, b, (((1,), (1,)), ((), ())), preferred_element_type=F32)


def _const_spec(shape):
    nd = len(shape)
    return pl.BlockSpec(shape, lambda *_: (0,) * nd)


def _memkv_kernel(mem_ref, nw_ref, wkv_ref, kv_ref):
    xn = _rms(mem_ref[0], nw_ref[...]).astype(BF16)
    kv_ref[0] = _dot(xn, wkv_ref[...]).astype(BF16)


def _memkv(mem, nw, wkv):
    b, m, d = mem.shape
    return pl.pallas_call(
        _memkv_kernel,
        grid=(b,),
        in_specs=[pl.BlockSpec((1, m, d), lambda i: (i, 0, 0)),
                  _const_spec((1, d)),
                  _const_spec((d, 2 * d))],
        out_specs=pl.BlockSpec((1, m, 2 * d), lambda i: (i, 0, 0)),
        out_shape=jax.ShapeDtypeStruct((b, m, 2 * d), BF16),
        compiler_params=_params(("parallel",)),
        name="memkv",
    )(mem, nw, wkv)


def _inproj_kernel(x_ref, nw_ref, win_ref, qnw_ref, kvnw_ref, wqm_ref, wqr_ref, wk_ref, wv_ref,
                   cos_ref, sin_ref, z_ref, xbc_ref, dt_ref, q_ref, k_ref, v_ref):
    hn = _rms(x_ref[...], nw_ref[...]).astype(BF16)
    proj = _dot(hn, win_ref[...])
    z_ref[...] = proj[:, O_Z:O_XBC]
    xbc_ref[...] = proj[:, O_XBC:O_CQ]
    dt_ref[...] = proj[:, O_DT:D_IN_PAD]
    cqn = _rms(proj[:, O_CQ:O_CKV], qnw_ref[...]).astype(BF16)
    ckvn = _rms(proj[:, O_CKV:O_KA], kvnw_ref[...]).astype(BF16)
    cos_t = cos_ref[...]
    sin_t = sin_ref[...]
    qscale = LOG2E * (QK_NOPE + QK_ROPE) ** -0.5
    cos_q = cos_t * qscale
    sin_q = sin_t * qscale
    qm = _dot(cqn, wqm_ref[...])
    qr = _dot(cqn, wqr_ref[...])
    kr = proj[:, O_KA:O_KB] * cos_t + proj[:, O_KB:O_DT] * sin_t
    km = _dot(ckvn, wk_ref[...])
    vm = _dot(ckvn, wv_ref[...])
    lane = lax.broadcasted_iota(jnp.int32, (1, HEAD_PAD), 1)
    one_col = jnp.where(lane == V_ONE_LANE, 1.0, 0.0).astype(F32)
    for h in range(MLA_HEADS):
        sl = slice(h * HEAD_PAD, (h + 1) * HEAD_PAD)
        q_ref[:, sl] = (qm[:, sl] * cos_q + qr[:, sl] * sin_q).astype(BF16)
        k_ref[:, sl] = (km[:, sl] + kr).astype(BF16)
        v_ref[:, sl] = (vm[:, sl] + one_col).astype(BF16)


def _inproj(x2, nw, win, qnw, kvnw, wqm, wqr, wk, wv, cos_t, sin_t, seq):
    n, d = x2.shape
    t = TOKEN_TILE
    tiles_per_seq = seq // t
    tok = lambda w: pl.BlockSpec((t, w), lambda i: (i, 0))
    pos = pl.BlockSpec((t, HEAD_PAD), lambda i: (i % tiles_per_seq, 0))
    hp = MLA_HEADS * HEAD_PAD
    outs = [jax.ShapeDtypeStruct((n, SSM_D), F32), jax.ShapeDtypeStruct((n, CONV_CH), F32),
            jax.ShapeDtypeStruct((n, LANE), F32), jax.ShapeDtypeStruct((n, hp), BF16),
            jax.ShapeDtypeStruct((n, hp), BF16), jax.ShapeDtypeStruct((n, hp), BF16)]
    return pl.pallas_call(
        _inproj_kernel,
        grid=(n // t,),
        in_specs=[tok(d), _const_spec((1, d)), _const_spec((d, D_IN_PAD)),
                  _const_spec((1, Q_LORA)), _const_spec((1, KV_LORA)),
                  _const_spec((Q_LORA, hp)), _const_spec((Q_LORA, hp)),
                  _const_spec((KV_LORA, hp)), _const_spec((KV_LORA, hp)),
                  pos, pos],
        out_specs=[tok(SSM_D), tok(CONV_CH), tok(LANE), tok(hp), tok(hp), tok(hp)],
        out_shape=outs,
        compiler_params=_params(("parallel",)),
        name="inproj",
    )(x2, nw, win, qnw, kvnw, wqm, wqr, wk, wv, cos_t, sin_t)


def _conv_silu(prev_ref, cur_ref, next_ref, has_prev, has_next, cw_ref, cb_ref):
    prev = jnp.where(has_prev, prev_ref[0], 0.0)
    nxt = jnp.where(has_next, next_ref[0], 0.0)
    ext = jnp.concatenate([prev, cur_ref[0], nxt], axis=0)
    acc = cb_ref[...] + jnp.zeros((CHUNK, CONV_CH), F32)
    base = SUBLANE - D_CONV // 2
    for j in range(D_CONV):
        acc = acc + ext[base + j:base + j + CHUNK, :] * cw_ref[j:j + 1, :]
    return _silu(acc)


def _softplus(x):
    return jnp.maximum(x, 0.0) + jnp.log1p(jnp.exp(-jnp.abs(x)))


def _ssd_chunk(xbc, dt_raw, dtb_ref, a_ref, h_ref, reverse, lane_off):
    L = CHUNK
    xs = xbc[:, :SSM_D]
    bm = xbc[:, SSM_D:SSM_D + SSM_GROUPS * SSM_STATE]
    cm = xbc[:, SSM_D + SSM_GROUPS * SSM_STATE:]
    dt = _softplus(dt_raw + dtb_ref[...])
    dta = dt * a_ref[...]
    row = lax.broadcasted_iota(jnp.int32, (L, L), 0)
    col = lax.broadcasted_iota(jnp.int32, (L, L), 1)
    mask = (col >= row) if reverse else (col <= row)
    tri = jnp.where(mask, 1.0, 0.0).astype(F32)
    cum = jnp.dot(tri, dta, preferred_element_type=F32, precision=lax.Precision.HIGHEST)
    cum_t = cum.T
    dt_t = dt.T
    total = cum[0:1, :] if reverse else cum[L - 1:L, :]
    w_state = dt * jnp.exp(total - cum)
    exp_cum = jnp.exp(cum)
    exp_total = jnp.exp(total)
    ys = []
    for g in range(SSM_GROUPS):
        bg = bm[:, g * SSM_STATE:(g + 1) * SSM_STATE]
        cg = cm[:, g * SSM_STATE:(g + 1) * SSM_STATE].astype(BF16)
        cb = _dot_nt(cg, bg.astype(BF16))
        h_prev = h_ref[g]
        y_off = _dot(cg, h_prev.astype(BF16))
        xw_parts = []
        dec_parts = []
        for r in range(SSM_HPG):
            hh = g * SSM_HPG + r
            ln = lane_off + hh
            x_h = xs[:, hh * SSM_HEAD_DIM:(hh + 1) * SSM_HEAD_DIM]
            seg = cum[:, ln:ln + 1] - cum_t[ln:ln + 1, :]
            dec = jnp.exp(jnp.where(mask, seg, -jnp.inf))
            m = (cb * dec * dt_t[ln:ln + 1, :]).astype(BF16)
            y_h = _dot(m, x_h.astype(BF16)) + \
                y_off[:, r * SSM_HEAD_DIM:(r + 1) * SSM_HEAD_DIM] * exp_cum[:, ln:ln + 1]
            ys.append(y_h)
            xw_parts.append(x_h * w_state[:, ln:ln + 1])
            dec_parts.append(jnp.broadcast_to(exp_total[:, ln:ln + 1], (1, SSM_HEAD_DIM)))
        xw = jnp.concatenate(xw_parts, axis=1).astype(BF16)
        st = _dot(bg.T.astype(BF16), xw)
        h_ref[g] = h_prev * jnp.concatenate(dec_parts, axis=1) + st
    return jnp.concatenate(ys, axis=1), xs


def _ssd_kernel(fp_ref, fc_ref, fn_ref, bp_ref, bc_ref, bn_ref, dtf_ref, dtb_ref,
                cw_ref, cb_ref, bias_ref, a_ref, dskip_ref, yf_ref, yb_ref, hf_ref, hb_ref):
    c = pl.program_id(1)
    nc = pl.num_programs(1)

    @pl.when(c == 0)
    def _():
        hf_ref[...] = jnp.zeros_like(hf_ref)
        hb_ref[...] = jnp.zeros_like(hb_ref)

    xbc_f = _conv_silu(fp_ref, fc_ref, fn_ref, c > 0, c < nc - 1, cw_ref, cb_ref)
    y_f, xs_f = _ssd_chunk(xbc_f, dtf_ref[0], bias_ref, a_ref, hf_ref, False, 0)
    yf_ref[0] = y_f + xs_f * dskip_ref[...]
    xbc_b = _conv_silu(bp_ref, bc_ref, bn_ref, c < nc - 1, c > 0, cw_ref, cb_ref)
    y_b, _ = _ssd_chunk(xbc_b, dtb_ref[0], bias_ref, a_ref, hb_ref, True, SSM_HEADS)
    yb_ref[0] = y_b


def _ssd(xbc, dt, cw, cb, bias, a_row, dskip):
    b, s, _ = xbc.shape
    nc = s // CHUNK
    rb = CHUNK // SUBLANE
    nrb = s // SUBLANE
    cur = lambda f: pl.BlockSpec((1, CHUNK, CONV_CH), lambda i, c: (i, f(c, nc), 0))
    prv = lambda f: pl.BlockSpec((1, SUBLANE, CONV_CH),
                                 lambda i, c: (i, jnp.maximum(f(c, nc) * rb - 1, 0), 0))
    nxt = lambda f: pl.BlockSpec((1, SUBLANE, CONV_CH),
                                 lambda i, c: (i, jnp.minimum(f(c, nc) * rb + rb, nrb - 1), 0))
    dts = lambda f: pl.BlockSpec((1, CHUNK, LANE), lambda i, c: (i, f(c, nc), 0))
    fwd = lambda c, n: c
    bwd = lambda c, n: n - 1 - c
    ysp = lambda f: pl.BlockSpec((1, CHUNK, SSM_D), lambda i, c: (i, f(c, nc), 0))
    hshape = (SSM_GROUPS, SSM_STATE, SSM_HPG * SSM_HEAD_DIM)
    return pl.pallas_call(
        _ssd_kernel,
        grid=(b, nc),
        in_specs=[prv(fwd), cur(fwd), nxt(fwd), prv(bwd), cur(bwd), nxt(bwd), dts(fwd), dts(bwd),
                  _const_spec((SUBLANE, CONV_CH)), _const_spec((1, CONV_CH)),
                  _const_spec((1, LANE)), _const_spec((1, LANE)), _const_spec((1, SSM_D))],
        out_specs=[ysp(fwd), ysp(bwd)],
        out_shape=[jax.ShapeDtypeStruct((b, s, SSM_D), F32)] * 2,
        scratch_shapes=[pltpu.VMEM(hshape, F32), pltpu.VMEM(hshape, F32)],
        compiler_params=_params(("parallel", "arbitrary")),
        name="ssd",
    )(xbc, xbc, xbc, xbc, xbc, xbc, dt, dt, cw, cb, bias, a_row, dskip)


def _flash_kernel(q_ref, k_ref, v_ref, o_ref, *, tk):
    s_len = k_ref.shape[1]
    tq = q_ref.shape[1]
    sls = [slice(hh * HEAD_PAD, (hh + 1) * HEAD_PAD) for hh in range(2)]

    def body(j, carry):
        off = pl.multiple_of(j * tk, tk)
        new = []
        for sl, (m, acc) in zip(sls, carry):
            kj = k_ref[0, pl.ds(off, tk), sl]
            vj = v_ref[0, pl.ds(off, tk), sl]
            s = _dot_nt(q_ref[0, :, sl], kj)
            m_new = jnp.maximum(m, jnp.max(s, axis=1, keepdims=True))
            alpha = jnp.exp2(m - m_new)
            p = jnp.exp2(s - m_new).astype(BF16)
            new.append((m_new, acc * alpha + _dot(p, vj)))
        return tuple(new)

    m0 = jnp.full((tq, 1), -jnp.inf, F32)
    acc0 = jnp.zeros((tq, HEAD_PAD), F32)
    carry = lax.fori_loop(0, s_len // tk, body, ((m0, acc0), (m0, acc0)), unroll=ATTN_UNROLL)
    outs = [acc / acc[:, V_ONE_LANE:V_ONE_LANE + 1] for _, acc in carry]
    lane = lax.broadcasted_iota(jnp.int32, (tq, HEAD_PAD), 1)
    o_ref[0] = jnp.where(lane < V_DIM, outs[0], pltpu.roll(outs[1], V_DIM, 1)).astype(BF16)


def _flash(q, k, v):
    b, s, hp = q.shape
    tq = min(ATTN_TQ, s)
    tk = min(ATTN_TK, s)
    pairs = MLA_HEADS // 2
    pw = 2 * HEAD_PAD
    return pl.pallas_call(
        functools.partial(_flash_kernel, tk=tk),
        grid=(b, pairs, s // tq),
        in_specs=[pl.BlockSpec((1, tq, pw), lambda i, p, j: (i, j, p)),
                  pl.BlockSpec((1, s, pw), lambda i, p, j: (i, 0, p)),
                  pl.BlockSpec((1, s, pw), lambda i, p, j: (i, 0, p))],
        out_specs=pl.BlockSpec((1, tq, 2 * V_DIM), lambda i, p, j: (i, j, p)),
        out_shape=jax.ShapeDtypeStruct((b, s, MLA_D), BF16),
        compiler_params=_params(("parallel", "parallel", "arbitrary")),
        name="mla_flash",
    )(q, k, v)


def _cross_attn(h1, kv_ref, pre_w, wq_ref, wo_ref, post_w):
    hn = _rms(h1, pre_w).astype(BF16)
    q = (_dot(hn, wq_ref[...]) * (XA_HEAD_DIM ** -0.5)).astype(BF16)
    heads = []
    for hd in range(XA_HEADS):
        sl = slice(hd * XA_HEAD_DIM, (hd + 1) * XA_HEAD_DIM)
        kh = kv_ref[0, :, sl]
        vh = kv_ref[0, :, D_MODEL + hd * XA_HEAD_DIM:D_MODEL + (hd + 1) * XA_HEAD_DIM]
        s = _dot_nt(q[:, sl], kh)
        p = jnp.exp(s - jnp.max(s, axis=1, keepdims=True))
        l = jnp.sum(p, axis=1, keepdims=True)
        heads.append((_dot(p.astype(BF16), vh) / l).astype(BF16))
    o = jnp.concatenate(heads, axis=1)
    xa = _dot(o, wo_ref[...])
    return h1 + _rms(xa, post_w)


def _postmix_even_kernel(h_ref, yf_ref, yb_ref, z_ref, o_ref, snw_ref, wout_ref, mpost_ref,
                         kv_ref, xpre_ref, wq_ref, wo_ref, xpost_ref, out_ref):
    y = (yf_ref[0] + yb_ref[0]) * _silu(z_ref[0])
    gw = SSM_D // SSM_GROUPS
    parts = []
    for g in range(SSM_GROUPS):
        yg = y[:, g * gw:(g + 1) * gw]
        parts.append(yg * lax.rsqrt(jnp.mean(yg * yg, axis=-1, keepdims=True) + EPS))
    y_ssd = (jnp.concatenate(parts, axis=1) * snw_ref[...]).astype(BF16)
    mix = _dot(y_ssd, wout_ref[:SSM_D, :]) + _dot(o_ref[0], wout_ref[SSM_D:, :])
    h1 = h_ref[0] + _rms(mix, mpost_ref[...])
    out_ref[0] = _cross_attn(h1, kv_ref, xpre_ref[...], wq_ref, wo_ref, xpost_ref[...])


def _postmix_odd_kernel(h_ref, f_ref, wmix_ref, mpost_ref,
                        kv_ref, xpre_ref, wq_ref, wo_ref, xpost_ref, out_ref):
    mix = _dot(f_ref[0], wmix_ref[...])
    h1 = h_ref[0] + _rms(mix, mpost_ref[...])
    out_ref[0] = _cross_attn(h1, kv_ref, xpre_ref[...], wq_ref, wo_ref, xpost_ref[...])


def _postmix(kernel, h, mixed, mixed_w, consts_a, kv, kv_off, consts_b):
    b, s, d = h.shape
    t = TOKEN_TILE
    tok = lambda w: pl.BlockSpec((1, t, w), lambda i, j: (i, j, 0))
    in_specs = [tok(d)] + [tok(w) for w in mixed_w]
    in_specs += [_const_spec(c.shape) for c in consts_a]
    in_specs += [pl.BlockSpec((1, N_MEM, 2 * d), lambda i, j: (i + kv_off, 0, 0))]
    in_specs += [_const_spec(c.shape) for c in consts_b]
    return pl.pallas_call(
        kernel,
        grid=(b, s // t),
        in_specs=in_specs,
        out_specs=tok(d),
        out_shape=jax.ShapeDtypeStruct((b, s, d), F32),
        compiler_params=_params(("parallel", "parallel")),
        name="postmix",
    )(h, *mixed, *consts_a, kv, *consts_b)


def _ffn_kernel(h_ref, pre_ref, wg_ref, wu_ref, wd_ref, post_ref, out_ref):
    h = h_ref[...]
    hn = _rms(h, pre_ref[...]).astype(BF16)
    d_ff = wg_ref.shape[1]
    acc = jnp.zeros(h.shape, F32)
    for c in range(d_ff // FFN_CHUNK):
        sl = slice(c * FFN_CHUNK, (c + 1) * FFN_CHUNK)
        g = _dot(hn, wg_ref[:, sl])
        u = _dot(hn, wu_ref[:, sl])
        acc = acc + _dot((_silu(g) * u).astype(BF16), wd_ref[sl, :])
    out_ref[...] = h + _rms(acc, post_ref[...])


def _ffn(h2, pre, wg, wu, wd, post):
    n, d = h2.shape
    t = TOKEN_TILE
    tok = pl.BlockSpec((t, d), lambda i: (i, 0))
    return pl.pallas_call(
        _ffn_kernel,
        grid=(n // t,),
        in_specs=[tok, _const_spec((1, d)), _const_spec(wg.shape), _const_spec(wu.shape),
                  _const_spec(wd.shape), _const_spec((1, d))],
        out_specs=tok,
        out_shape=jax.ShapeDtypeStruct((n, d), F32),
        compiler_params=_params(("parallel",)),
        name="ffn",
    )(h2, pre, wg, wu, wd, post)


def _fnet_a_kernel(x_ref, nw_ref, cs_ref, m1_ref, a_ref):
    gd = FOURIER_GROUP_DIM
    for c in range(x_ref.shape[2] // D_MODEL):
        cl = slice(c * D_MODEL, (c + 1) * D_MODEL)
        xn = _rms(x_ref[0, :, cl], nw_ref[...]).astype(BF16)
        yr, yi = [], []
        for g in range(FOURIER_GROUPS):
            y = _dot(xn[:, g * gd:(g + 1) * gd], cs_ref[...])
            yr.append(y[:, :gd])
            yi.append(y[:, gd:])
        stack = jnp.concatenate([jnp.concatenate(yr, axis=1), jnp.concatenate(yi, axis=1)], axis=0)
        a_ref[0, :, cl] = _dot(m1_ref[...], stack.astype(BF16)).astype(BF16)


def _fnet_b_kernel(a_ref, g_ref, f_ref):
    for c in range(g_ref.shape[0]):
        stack = jnp.concatenate([a_ref[0, 0, c], a_ref[0, 1, c]], axis=0)
        f_ref[0, :, c * D_MODEL:(c + 1) * D_MODEL] = _dot(g_ref[c], stack).astype(BF16)


def _fnet(h, nw, cs, m1, gt):
    b, s, d = h.shape
    n1 = m1.shape[0] // 2
    n2 = s // n1
    ca = min(FNET_COLS, n2)
    cb = min(FNET_COLS, n1)
    a = pl.pallas_call(
        _fnet_a_kernel,
        grid=(b, n2 // ca),
        in_specs=[pl.BlockSpec((1, n1, ca * d), lambda i, j: (i, 0, j)),
                  _const_spec((1, d)), _const_spec(cs.shape), _const_spec(m1.shape)],
        out_specs=pl.BlockSpec((1, 2 * n1, ca * d), lambda i, j: (i, 0, j)),
        out_shape=jax.ShapeDtypeStruct((b, 2 * n1, n2 * d), BF16),
        compiler_params=_params(("parallel", "parallel")),
        name="fnet_a",
    )(h.reshape(b, n1, n2 * d), nw, cs, m1)
    f = pl.pallas_call(
        _fnet_b_kernel,
        grid=(b, n1 // cb),
        in_specs=[pl.BlockSpec((1, 2, cb, n2, d), lambda i, j: (i, 0, j, 0, 0)),
                  pl.BlockSpec((cb, n2, 2 * n2), lambda i, j: (j, 0, 0))],
        out_specs=pl.BlockSpec((1, n2, cb * d), lambda i, j: (i, 0, j)),
        out_shape=jax.ShapeDtypeStruct((b, n2, n1 * d), BF16),
        compiler_params=_params(("parallel", "parallel")),
        name="fnet_b",
    )(a.reshape(b, 2, n1, n2, d), gt)
    return f.reshape(b, s, d)


def _rope_tables(s):
    inv = ROPE_THETA ** (-jnp.arange(0, QK_ROPE, 2, dtype=F32) / QK_ROPE)
    ang = jnp.arange(s, dtype=F32)[:, None] * inv[None, :]
    cos2 = jnp.concatenate([jnp.cos(ang), jnp.cos(ang)], axis=1)
    sin2 = jnp.concatenate([jnp.sin(ang), jnp.sin(ang)], axis=1)
    pad = HEAD_PAD - QK_NOPE - QK_ROPE
    cos_t = jnp.concatenate([jnp.ones((s, QK_NOPE), F32), cos2, jnp.ones((s, pad), F32)], axis=1)
    sin_t = jnp.concatenate([jnp.zeros((s, QK_NOPE), F32), sin2, jnp.zeros((s, pad), F32)], axis=1)
    return cos_t, sin_t


def _rot_cols(w):
    half = w.shape[-1] // 2
    return jnp.concatenate([-w[..., half:], w[..., :half]], axis=-1)


def _pad_cols(w, left, total):
    return jnp.pad(w, ((0, 0), (left, total - left - w.shape[1])))


def _even_weights(w_in, w_uq, w_ukv):
    o1 = SSM_D
    o2 = o1 + CONV_CH
    o3 = o2 + 2 * SSM_HEADS
    o4 = o3 + Q_LORA
    o5 = o4 + KV_LORA
    w_z, w_xbc, w_dt, w_cq, w_ckv, w_kr = (w_in[:, :o1], w_in[:, o1:o2], w_in[:, o2:o3],
                                             w_in[:, o3:o4], w_in[:, o4:o5], w_in[:, o5:])
    win = jnp.concatenate([
        w_z, w_xbc, w_cq, w_ckv,
        _pad_cols(w_kr, QK_NOPE, LANE), _pad_cols(_rot_cols(w_kr), QK_NOPE, LANE),
        _pad_cols(w_dt, 0, LANE)], axis=1).astype(BF16)
    dq = QK_NOPE + QK_ROPE
    wq = w_uq.reshape(Q_LORA, MLA_HEADS, dq)
    zq = jnp.zeros((Q_LORA, MLA_HEADS, HEAD_PAD - dq), F32)
    wqm = jnp.concatenate([wq, zq], axis=-1).reshape(Q_LORA, -1).astype(BF16)
    wqr = jnp.concatenate([jnp.zeros((Q_LORA, MLA_HEADS, QK_NOPE), F32),
                           _rot_cols(wq[..., QK_NOPE:]), zq], axis=-1).reshape(Q_LORA, -1).astype(BF16)
    wkv = w_ukv.reshape(KV_LORA, MLA_HEADS, QK_NOPE + V_DIM)
    zk = jnp.zeros((KV_LORA, MLA_HEADS, HEAD_PAD - QK_NOPE), F32)
    wk = jnp.concatenate([wkv[..., :QK_NOPE], zk], axis=-1).reshape(KV_LORA, -1).astype(BF16)
    zv = jnp.zeros((KV_LORA, MLA_HEADS, HEAD_PAD - V_DIM), F32)
    wv = jnp.concatenate([wkv[..., QK_NOPE:], zv], axis=-1).reshape(KV_LORA, -1).astype(BF16)
    return win, wqm, wqr, wk, wv


def _fnet_tables(s):
    n2 = CHUNK
    n1 = s // n2
    gd = FOURIER_GROUP_DIM
    ci = jnp.arange(gd, dtype=jnp.int32)
    ang_c = (2.0 * math.pi / gd) * ((ci[:, None] * ci[None, :]) % gd).astype(F32)
    cs = (jnp.concatenate([jnp.cos(ang_c), -jnp.sin(ang_c)], axis=1) * gd ** -0.5).astype(BF16)
    i1 = jnp.arange(n1, dtype=jnp.int32)
    ang1 = (2.0 * math.pi / n1) * ((i1[:, None] * i1[None, :]) % n1).astype(F32)
    c1, s1 = jnp.cos(ang1), jnp.sin(ang1)
    m1 = jnp.concatenate([jnp.concatenate([c1, s1], axis=1),
                          jnp.concatenate([-s1, c1], axis=1)], axis=0).astype(BF16)
    k1 = jnp.arange(n1, dtype=jnp.int32)[:, None, None]
    k2 = jnp.arange(n2, dtype=jnp.int32)[None, :, None]
    j2 = jnp.arange(n2, dtype=jnp.int32)[None, None, :]
    ang = (2.0 * math.pi / s) * ((j2 * (k1 + n1 * k2)) % s).astype(F32)
    gt = (jnp.concatenate([jnp.cos(ang), jnp.sin(ang)], axis=2) * s ** -0.5).astype(BF16)
    return cs, m1, gt


def _row(v, width=None):
    v = v.astype(F32).reshape(1, -1)
    if width is not None:
        v = jnp.pad(v, ((0, 0), (0, width - v.shape[1])))
    return v


def _trunk(x, kv_layers, kv_off, p):
    b, s, d = x.shape
    n = b * s
    h = x
    z, xbc, dt, q, k, v = _inproj(h.reshape(n, d), p["mix_pre"][0], p["win"], p["q_norm"], p["kv_norm"],
                                  p["wqm"], p["wqr"], p["wk"], p["wv"], p["cos"][:s], p["sin"][:s], s)
    yf, yb = _ssd(xbc.reshape(b, s, -1), dt.reshape(b, s, -1), p["conv_w"], p["conv_b"],
                  p["dt_bias"], p["a_row"], p["d_skip"])
    hp = MLA_HEADS * HEAD_PAD
    o = _flash(q.reshape(b, s, hp), k.reshape(b, s, hp), v.reshape(b, s, hp))
    h = _postmix(_postmix_even_kernel, h, (yf, yb, z.reshape(b, s, -1), o),
                 (SSM_D, SSM_D, SSM_D, MLA_D),
                 (p["ssm_norm"], p["w_out"], p["mix_post"][0]), kv_layers[0], kv_off,
                 (p["xa_pre"][0], p["xa_wq"][0], p["xa_wo"][0], p["xa_post"][0]))
    h = _ffn(h.reshape(n, d), p["ffn_pre"][0], p["wg"][0], p["wu"][0], p["wd"][0],
             p["ffn_post"][0]).reshape(b, s, d)
    cs, m1, gt = _fnet_tables(s)
    f = _fnet(h, p["mix_pre"][1], cs, m1, gt)
    h = _postmix(_postmix_odd_kernel, h, (f,), (d,), (p["w_mix"], p["mix_post"][1]),
                 kv_layers[1], kv_off,
                 (p["xa_pre"][1], p["xa_wq"][1], p["xa_wo"][1], p["xa_post"][1]))
    h = _ffn(h.reshape(n, d), p["ffn_pre"][1], p["wg"][1], p["wu"][1], p["wd"][1],
             p["ffn_post"][1]).reshape(b, s, d)
    return h


def kernel(x_prompt, x_sample, mem_prompt, mem_sample, norm_mix_pre, norm_mix_post, norm_xa_pre, norm_xa_post, norm_mem, xa_wq, xa_wkv, xa_wo, norm_ffn_pre, norm_ffn_post, ffn_w_gu, ffn_w_down, ev_w_in, ev_conv_w, ev_conv_b, ev_a_log_f, ev_a_log_b, ev_dt_bias_f, ev_dt_bias_b, ev_d_skip, ev_ssm_norm, ev_q_norm, ev_w_uq, ev_kv_norm, ev_w_ukv, ev_w_out, od_w_mix):
    depth = norm_mix_pre.shape[0]
    d_ff = ffn_w_down.shape[1]
    s_max = max(x_prompt.shape[1], x_sample.shape[1])
    cos_t, sin_t = _rope_tables(s_max)
    win, wqm, wqr, wk, wv = _even_weights(ev_w_in[0], ev_w_uq[0], ev_w_ukv[0])
    rows = lambda w: [_row(w[i]) for i in range(depth)]
    p = {
        "mix_pre": rows(norm_mix_pre), "mix_post": rows(norm_mix_post),
        "xa_pre": rows(norm_xa_pre), "xa_post": rows(norm_xa_post),
        "ffn_pre": rows(norm_ffn_pre), "ffn_post": rows(norm_ffn_post),
        "xa_wq": [xa_wq[i].astype(BF16) for i in range(depth)],
        "xa_wo": [xa_wo[i].astype(BF16) for i in range(depth)],
        "wg": [ffn_w_gu[i, :, :d_ff].astype(BF16) for i in range(depth)],
        "wu": [ffn_w_gu[i, :, d_ff:].astype(BF16) for i in range(depth)],
        "wd": [ffn_w_down[i].astype(BF16) for i in range(depth)],
        "win": win, "wqm": wqm, "wqr": wqr, "wk": wk, "wv": wv,
        "q_norm": _row(ev_q_norm[0]), "kv_norm": _row(ev_kv_norm[0]),
        "cos": cos_t, "sin": sin_t,
        "conv_w": jnp.pad(ev_conv_w[0].astype(F32), ((0, SUBLANE - D_CONV), (0, 0))),
        "conv_b": _row(ev_conv_b[0]),
        "dt_bias": _row(jnp.concatenate([ev_dt_bias_f[0], ev_dt_bias_b[0]]), LANE),
        "a_row": _row(-jnp.exp(jnp.concatenate([ev_a_log_f[0], ev_a_log_b[0]]).astype(F32)), LANE),
        "d_skip": _row(jnp.repeat(ev_d_skip[0].astype(F32), SSM_HEAD_DIM)),
        "ssm_norm": _row(ev_ssm_norm[0]),
        "w_out": ev_w_out[0].astype(BF16),
        "w_mix": od_w_mix[0].astype(BF16),
    }
    mem = jnp.concatenate([mem_prompt, mem_sample], axis=0)
    kv_layers = [_memkv(mem, _row(norm_mem[i]), xa_wkv[i].astype(BF16)) for i in range(depth)]
    y_prompt = _trunk(x_prompt, kv_layers, 0, p)
    y_sample = _trunk(x_sample, kv_layers, x_prompt.shape[0], p)
    return (y_prompt, y_sample)
```

```python
import functools
import math

import jax
import jax.numpy as jnp
from jax import lax
from jax.experimental import pallas as pl
from jax.experimental.pallas import tpu as pltpu

F32 = jnp.float32
BF16 = jnp.bfloat16

EPS = 1e-6
D_MODEL = 1024
N_MEM = 256

SSM_HEADS = 8
SSM_HEAD_DIM = 64
SSM_D = SSM_HEADS * SSM_HEAD_DIM
SSM_GROUPS = 2
SSM_HPG = SSM_HEADS // SSM_GROUPS
SSM_STATE = 128
D_CONV = 5
CONV_CH = SSM_D + 2 * SSM_GROUPS * SSM_STATE
CHUNK = 128

MLA_HEADS = 8
QK_NOPE = 64
QK_ROPE = 32
V_DIM = 64
Q_LORA = 256
KV_LORA = 128
ROPE_THETA = 10000.0
MLA_D = MLA_HEADS * V_DIM

FOURIER_GROUPS = 4
FOURIER_GROUP_DIM = D_MODEL // FOURIER_GROUPS

XA_HEADS = 4
XA_HEAD_DIM = D_MODEL // XA_HEADS

LANE = 128
SUBLANE = 8
HEAD_PAD = LANE
V_ONE_LANE = V_DIM
VMEM_LIMIT = 56 * 1024 * 1024

TOKEN_TILE = 512
FFN_CHUNK = 256
ATTN_TQ = 1024
ATTN_TK = 512
ATTN_UNROLL = 4
FNET_COLS = 8
SSD_CHUNKS_PER_STEP = 2
LOG2E = 1.4426950408889634

O_Z = 0
O_XBC = O_Z + SSM_D
O_CQ = O_XBC + CONV_CH
O_CKV = O_CQ + Q_LORA
O_KA = O_CKV + KV_LORA
O_KB = O_KA + LANE
O_DT = O_KB + LANE
D_IN_PAD = O_DT + LANE


def _params(sem, vmem=VMEM_LIMIT):
    return pltpu.CompilerParams(dimension_semantics=sem, vmem_limit_bytes=vmem)


def _rms(x, w):
    ms = jnp.mean(x * x, axis=-1, keepdims=True)
    return x * lax.rsqrt(ms + EPS) * w


def _silu(x):
    return x / (1.0 + jnp.exp(-x))


def _dot(a, b):
    return jnp.dot(a, b, preferred_element_type=F32)


def _dot_nt(a, b):
    return lax.dot_general(a, b, (((1,), (1,)), ((), ())), preferred_element_type=F32)


def _const_spec(shape):
    nd = len(shape)
    return pl.BlockSpec(shape, lambda *_: (0,) * nd)


def _memkv_kernel(mem_ref, nw_ref, wkv_ref, kv_ref):
    xn = _rms(mem_ref[0], nw_ref[...]).astype(BF16)
    kv_ref[0] = _dot(xn, wkv_ref[...]).astype(BF16)


def _memkv(mem, nw, wkv):
    b, m, d = mem.shape
    return pl.pallas_call(
        _memkv_kernel,
        grid=(b,),
        in_specs=[pl.BlockSpec((1, m, d), lambda i: (i, 0, 0)),
                  _const_spec((1, d)),
                  _const_spec((d, 2 * d))],
        out_specs=pl.BlockSpec((1, m, 2 * d), lambda i: (i, 0, 0)),
        out_shape=jax.ShapeDtypeStruct((b, m, 2 * d), BF16),
        compiler_params=_params(("parallel",)),
        name="memkv",
    )(mem, nw, wkv)


def _inproj_kernel(xp_ref, x_ref, xn_ref, nw_ref, win_ref, qnw_ref, kvnw_ref, wqm_ref, wqr_ref,
                   wk_ref, wv_ref, cos_ref, sin_ref, cw_ref, cb_ref,
                   z_ref, xbc_ref, dt_ref, q_ref, k_ref, v_ref, *, tiles_per_seq):
    t = x_ref.shape[0]
    pos = pl.program_id(0) % tiles_per_seq
    x_ext = jnp.concatenate([xp_ref[...], x_ref[...], xn_ref[...]], axis=0)
    hn = _rms(x_ext, nw_ref[...]).astype(BF16)
    proj_ext = _dot(hn, win_ref[...])
    proj = proj_ext[SUBLANE:SUBLANE + t, :]
    z_ref[...] = proj[:, O_Z:O_XBC]
    dt_ref[...] = proj[:, O_DT:D_IN_PAD]
    row = lax.broadcasted_iota(jnp.int32, (t + 2 * SUBLANE, 1), 0)
    inside = ((row >= SUBLANE) | (pos > 0)) & ((row < SUBLANE + t) | (pos < tiles_per_seq - 1))
    xbc_ext = jnp.where(inside, proj_ext[:, O_XBC:O_CQ], 0.0)
    acc = cb_ref[...] + jnp.zeros((t, CONV_CH), F32)
    n_ext = t + 2 * SUBLANE
    for j in range(D_CONV):
        tap = pltpu.roll(xbc_ext, (D_CONV // 2 - j) % n_ext, 0)[SUBLANE:SUBLANE + t, :]
        acc = acc + tap * cw_ref[j:j + 1, :]
    xbc_ref[...] = _silu(acc).astype(BF16)
    cqn = _rms(proj[:, O_CQ:O_CKV], qnw_ref[...]).astype(BF16)
    ckvn = _rms(proj[:, O_CKV:O_KA], kvnw_ref[...]).astype(BF16)
    cos_t = cos_ref[...]
    sin_t = sin_ref[...]
    qscale = LOG2E * (QK_NOPE + QK_ROPE) ** -0.5
    cos_q = cos_t * qscale
    sin_q = sin_t * qscale
    qm = _dot(cqn, wqm_ref[...])
    qr = _dot(cqn, wqr_ref[...])
    kr = proj[:, O_KA:O_KB] * cos_t + proj[:, O_KB:O_DT] * sin_t
    km = _dot(ckvn, wk_ref[...])
    vm = _dot(ckvn, wv_ref[...])
    lane = lax.broadcasted_iota(jnp.int32, (1, HEAD_PAD), 1)
    one_col = jnp.where(lane == V_ONE_LANE, 1.0, 0.0).astype(F32)
    for h in range(MLA_HEADS):
        sl = slice(h * HEAD_PAD, (h + 1) * HEAD_PAD)
        q_ref[:, sl] = (qm[:, sl] * cos_q + qr[:, sl] * sin_q).astype(BF16)
        k_ref[:, sl] = (km[:, sl] + kr).astype(BF16)
        v_ref[:, sl] = (vm[:, sl] + one_col).astype(BF16)


def _inproj(x2, nw, win, qnw, kvnw, wqm, wqr, wk, wv, cos_t, sin_t, cw, cb, seq):
    n, d = x2.shape
    t = TOKEN_TILE
    tiles_per_seq = seq // t
    rb = t // SUBLANE
    nrb = n // SUBLANE
    tok = lambda w: pl.BlockSpec((t, w), lambda i: (i, 0))
    halo_prev = pl.BlockSpec((SUBLANE, d), lambda i: (jnp.maximum(i * rb - 1, 0), 0))
    halo_next = pl.BlockSpec((SUBLANE, d), lambda i: (jnp.minimum(i * rb + rb, nrb - 1), 0))
    pos = pl.BlockSpec((t, HEAD_PAD), lambda i: (i % tiles_per_seq, 0))
    hp = MLA_HEADS * HEAD_PAD
    outs = [jax.ShapeDtypeStruct((n, SSM_D), F32), jax.ShapeDtypeStruct((n, CONV_CH), BF16),
            jax.ShapeDtypeStruct((n, LANE), F32), jax.ShapeDtypeStruct((n, hp), BF16),
            jax.ShapeDtypeStruct((n, hp), BF16), jax.ShapeDtypeStruct((n, hp), BF16)]
    return pl.pallas_call(
        functools.partial(_inproj_kernel, tiles_per_seq=tiles_per_seq),
        grid=(n // t,),
        in_specs=[halo_prev, tok(d), halo_next, _const_spec((1, d)), _const_spec((d, D_IN_PAD)),
                  _const_spec((1, Q_LORA)), _const_spec((1, KV_LORA)),
                  _const_spec((Q_LORA, hp)), _const_spec((Q_LORA, hp)),
                  _const_spec((KV_LORA, hp)), _const_spec((KV_LORA, hp)),
                  pos, pos, _const_spec((SUBLANE, CONV_CH)), _const_spec((1, CONV_CH))],
        out_specs=[tok(SSM_D), tok(CONV_CH), tok(LANE), tok(hp), tok(hp), tok(hp)],
        out_shape=outs,
        compiler_params=_params(("parallel",)),
        name="inproj",
    )(x2, x2, x2, nw, win, qnw, kvnw, wqm, wqr, wk, wv, cos_t, sin_t, cw, cb)


def _softplus(x):
    return jnp.maximum(x, 0.0) + jnp.log1p(jnp.exp(-jnp.abs(x)))


def _dot_pieces(a, passes, fn):
    out = None
    rem = a
    for _ in range(passes):
        piece = rem.astype(BF16)
        rem = rem - piece.astype(F32)
        term = fn(piece)
        out = term if out is None else out + term
    return out


def _ssd_chunk(xbc, dt_raw, dtb_ref, a_ref, e64_ref, h_in, reverse, lane_off):
    L = CHUNK
    P = SSM_HEAD_DIM
    gw = SSM_HPG * P
    xs = xbc[:, :SSM_D].astype(F32)
    bm = xbc[:, SSM_D:SSM_D + SSM_GROUPS * SSM_STATE]
    cm = xbc[:, SSM_D + SSM_GROUPS * SSM_STATE:]
    dt = _softplus(dt_raw + dtb_ref[...])
    dta = dt * a_ref[...]
    row = lax.broadcasted_iota(jnp.int32, (L, L), 0)
    col = lax.broadcasted_iota(jnp.int32, (L, L), 1)
    mask = (col >= row) if reverse else (col <= row)
    tri = jnp.where(mask, 1.0, 0.0).astype(BF16)
    cum = _dot_pieces(dta, 3, lambda piece: _dot(tri, piece))
    cum_t = cum.T
    dt_t = dt.T
    last = 0 if reverse else L - 1
    exp_cum = jnp.exp(cum)
    w_state = dt * jnp.exp(cum[last:last + 1, :] - cum)
    scal64 = _dot_pieces(jnp.concatenate([w_state, exp_cum], axis=0), 2,
                         lambda piece: _dot(piece, e64_ref[...]))
    xw = (xs * scal64[:L]).astype(BF16)
    exp_cum64 = scal64[L:]
    exp_total64 = exp_cum64[last:last + 1, :]
    lane = lax.broadcasted_iota(jnp.int32, (L, 2 * P), 1)
    ys = []
    h_out = []
    for g in range(SSM_GROUPS):
        bg = bm[:, g * SSM_STATE:(g + 1) * SSM_STATE]
        cg = cm[:, g * SSM_STATE:(g + 1) * SSM_STATE]
        cb = _dot_nt(cg, bg)
        h_prev = h_in[g]
        y_off = _dot(cg, h_prev.astype(BF16)) * exp_cum64[:, g * gw:(g + 1) * gw]
        pairs = []
        for pr in range(SSM_HPG // 2):
            x_pair = xbc[:, g * gw + pr * 2 * P:g * gw + (pr + 1) * 2 * P]
            halves = []
            for q in range(2):
                hh = g * SSM_HPG + pr * 2 + q
                ln = lane_off + hh
                seg = cum[:, ln:ln + 1] - cum_t[ln:ln + 1, :]
                dec = jnp.exp(jnp.where(mask, seg, -jnp.inf))
                m = (cb * dec * dt_t[ln:ln + 1, :]).astype(BF16)
                halves.append(_dot(m, x_pair))
            pairs.append(jnp.where(lane < P, halves[0], halves[1]))
        ys.append(y_off + jnp.concatenate(pairs, axis=1))
        st = _dot(bg.astype(F32).T.astype(BF16), xw[:, g * gw:(g + 1) * gw])
        h_out.append(h_prev * exp_total64[:, g * gw:(g + 1) * gw] + st)
    return jnp.concatenate(ys, axis=1), xs, h_out


def _ssd_kernel(xf_ref, xb_ref, dtf_ref, dtb_ref, bias_ref, a_ref, dskip_ref,
                e64f_ref, e64b_ref, yf_ref, yb_ref, hf_ref, hb_ref):
    @pl.when(pl.program_id(1) == 0)
    def _():
        hf_ref[...] = jnp.zeros_like(hf_ref)
        hb_ref[...] = jnp.zeros_like(hb_ref)

    n_sub = xf_ref.shape[1] // CHUNK
    h_f = [hf_ref[g] for g in range(SSM_GROUPS)]
    h_b = [hb_ref[g] for g in range(SSM_GROUPS)]
    for i in range(n_sub):
        rows = slice(i * CHUNK, (i + 1) * CHUNK)
        y_f, xs_f, h_f = _ssd_chunk(xf_ref[0, rows, :], dtf_ref[0, rows, :], bias_ref, a_ref,
                                    e64f_ref, h_f, False, 0)
        yf_ref[0, rows, :] = y_f + xs_f * dskip_ref[...]
        rows = slice((n_sub - 1 - i) * CHUNK, (n_sub - i) * CHUNK)
        y_b, _, h_b = _ssd_chunk(xb_ref[0, rows, :], dtb_ref[0, rows, :], bias_ref, a_ref,
                                 e64b_ref, h_b, True, SSM_HEADS)
        yb_ref[0, rows, :] = y_b
    for g in range(SSM_GROUPS):
        hf_ref[g] = h_f[g]
        hb_ref[g] = h_b[g]


def _head_lane_expander(lane_off, width):
    src_lane = lax.broadcasted_iota(jnp.int32, (LANE, SSM_HEADS * width), 0)
    dst_head = lax.broadcasted_iota(jnp.int32, (LANE, SSM_HEADS * width), 1) // width
    return (src_lane == dst_head + lane_off).astype(BF16)


def _ssd(xbc, dt, bias, a_row, dskip):
    b, s, _ = xbc.shape
    expanders = [_head_lane_expander(off, SSM_HEAD_DIM) for off in (0, SSM_HEADS)]
    rows = SSD_CHUNKS_PER_STEP * CHUNK
    nc = s // rows
    chunk = lambda w, f: pl.BlockSpec((1, rows, w), lambda i, c: (i, f(c), 0))
    fwd = lambda c: c
    bwd = lambda c: nc - 1 - c
    hshape = (SSM_GROUPS, SSM_STATE, SSM_HPG * SSM_HEAD_DIM)
    return pl.pallas_call(
        _ssd_kernel,
        grid=(b, nc),
        in_specs=[chunk(CONV_CH, fwd), chunk(CONV_CH, bwd), chunk(LANE, fwd), chunk(LANE, bwd),
                  _const_spec((1, LANE)), _const_spec((1, LANE)), _const_spec((1, SSM_D))]
        + [_const_spec(e.shape) for e in expanders],
        out_specs=[chunk(SSM_D, fwd), chunk(SSM_D, bwd)],
        out_shape=[jax.ShapeDtypeStruct((b, s, SSM_D), F32)] * 2,
        scratch_shapes=[pltpu.VMEM(hshape, F32), pltpu.VMEM(hshape, F32)],
        compiler_params=_params(("parallel", "arbitrary")),
        name="ssd",
    )(xbc, xbc, dt, dt, bias, a_row, dskip, *expanders)


def _flash_kernel(q_ref, k_ref, v_ref, o_ref, *, tk):
    s_len = k_ref.shape[1]
    tq = q_ref.shape[1]
    sls = [slice(hh * HEAD_PAD, (hh + 1) * HEAD_PAD) for hh in range(2)]

    def body(j, carry):
        off = pl.multiple_of(j * tk, tk)
        new = []
        for sl, (m, acc) in zip(sls, carry):
            kj = k_ref[0, pl.ds(off, tk), sl]
            vj = v_ref[0, pl.ds(off, tk), sl]
            s = _dot_nt(q_ref[0, :, sl], kj)
            m_new = jnp.maximum(m, jnp.max(s, axis=1, keepdims=True))
            alpha = jnp.exp2(m - m_new)
            p = jnp.exp2(s - m_new).astype(BF16)
            new.append((m_new, acc * alpha + _dot(p, vj)))
        return tuple(new)

    m0 = jnp.full((tq, 1), -jnp.inf, F32)
    acc0 = jnp.zeros((tq, HEAD_PAD), F32)
    carry = lax.fori_loop(0, s_len // tk, body, ((m0, acc0), (m0, acc0)), unroll=ATTN_UNROLL)
    outs = [acc / acc[:, V_ONE_LANE:V_ONE_LANE + 1] for _, acc in carry]
    lane = lax.broadcasted_iota(jnp.int32, (tq, HEAD_PAD), 1)
    o_ref[0] = jnp.where(lane < V_DIM, outs[0], pltpu.roll(outs[1], V_DIM, 1)).astype(BF16)


def _flash(q, k, v):
    b, s, hp = q.shape
    tq = min(ATTN_TQ, s)
    tk = min(ATTN_TK, s)
    pairs = MLA_HEADS // 2
    pw = 2 * HEAD_PAD
    return pl.pallas_call(
        functools.partial(_flash_kernel, tk=tk),
        grid=(b, pairs, s // tq),
        in_specs=[pl.BlockSpec((1, tq, pw), lambda i, p, j: (i, j, p)),
                  pl.BlockSpec((1, s, pw), lambda i, p, j: (i, 0, p)),
                  pl.BlockSpec((1, s, pw), lambda i, p, j: (i, 0, p))],
        out_specs=pl.BlockSpec((1, tq, 2 * V_DIM), lambda i, p, j: (i, j, p)),
        out_shape=jax.ShapeDtypeStruct((b, s, MLA_D), BF16),
        compiler_params=_params(("parallel", "parallel", "arbitrary")),
        name="mla_flash",
    )(q, k, v)


def _cross_attn(h1, kv_ref, pre_w, wq_ref, wo_ref, post_w):
    hn = _rms(h1, pre_w).astype(BF16)
    q = (_dot(hn, wq_ref[...]) * (XA_HEAD_DIM ** -0.5)).astype(BF16)
    heads = []
    for hd in range(XA_HEADS):
        sl = slice(hd * XA_HEAD_DIM, (hd + 1) * XA_HEAD_DIM)
        kh = kv_ref[0, :, sl]
        vh = kv_ref[0, :, D_MODEL + hd * XA_HEAD_DIM:D_MODEL + (hd + 1) * XA_HEAD_DIM]
        s = _dot_nt(q[:, sl], kh)
        p = jnp.exp(s - jnp.max(s, axis=1, keepdims=True))
        l = jnp.sum(p, axis=1, keepdims=True)
        heads.append((_dot(p.astype(BF16), vh) / l).astype(BF16))
    o = jnp.concatenate(heads, axis=1)
    xa = _dot(o, wo_ref[...])
    return h1 + _rms(xa, post_w)


def _postmix_even_kernel(h_ref, yf_ref, yb_ref, z_ref, o_ref, snw_ref, wout_ref, mpost_ref,
                         kv_ref, xpre_ref, wq_ref, wo_ref, xpost_ref, out_ref):
    y = (yf_ref[0] + yb_ref[0]) * _silu(z_ref[0])
    gw = SSM_D // SSM_GROUPS
    parts = []
    for g in range(SSM_GROUPS):
        yg = y[:, g * gw:(g + 1) * gw]
        parts.append(yg * lax.rsqrt(jnp.mean(yg * yg, axis=-1, keepdims=True) + EPS))
    y_ssd = (jnp.concatenate(parts, axis=1) * snw_ref[...]).astype(BF16)
    mix = _dot(y_ssd, wout_ref[:SSM_D, :]) + _dot(o_ref[0], wout_ref[SSM_D:, :])
    h1 = h_ref[0] + _rms(mix, mpost_ref[...])
    out_ref[0] = _cross_attn(h1, kv_ref, xpre_ref[...], wq_ref, wo_ref, xpost_ref[...])


def _postmix_odd_kernel(h_ref, f_ref, wmix_ref, mpost_ref,
                        kv_ref, xpre_ref, wq_ref, wo_ref, xpost_ref, out_ref):
    mix = _dot(f_ref[0], wmix_ref[...])
    h1 = h_ref[0] + _rms(mix, mpost_ref[...])
    out_ref[0] = _cross_attn(h1, kv_ref, xpre_ref[...], wq_ref, wo_ref, xpost_ref[...])


def _postmix(kernel, h, mixed, mixed_w, consts_a, kv, kv_off, consts_b):
    b, s, d = h.shape
    t = TOKEN_TILE
    tok = lambda w: pl.BlockSpec((1, t, w), lambda i, j: (i, j, 0))
    in_specs = [tok(d)] + [tok(w) for w in mixed_w]
    in_specs += [_const_spec(c.shape) for c in consts_a]
    in_specs += [pl.BlockSpec((1, N_MEM, 2 * d), lambda i, j: (i + kv_off, 0, 0))]
    in_specs += [_const_spec(c.shape) for c in consts_b]
    return pl.pallas_call(
        kernel,
        grid=(b, s // t),
        in_specs=in_specs,
        out_specs=tok(d),
        out_shape=jax.ShapeDtypeStruct((b, s, d), F32),
        compiler_params=_params(("parallel", "parallel")),
        name="postmix",
    )(h, *mixed, *consts_a, kv, *consts_b)


def _ffn_kernel(h_ref, pre_ref, wg_ref, wu_ref, wd_ref, post_ref, out_ref):
    h = h_ref[...]
    hn = _rms(h, pre_ref[...]).astype(BF16)
    d_ff = wg_ref.shape[1]
    acc = jnp.zeros(h.shape, F32)
    for c in range(d_ff // FFN_CHUNK):
        sl = slice(c * FFN_CHUNK, (c + 1) * FFN_CHUNK)
        g = _dot(hn, wg_ref[:, sl])
        u = _dot(hn, wu_ref[:, sl])
        acc = acc + _dot((_silu(g) * u).astype(BF16), wd_ref[sl, :])
    out_ref[...] = h + _rms(acc, post_ref[...])


def _ffn(h2, pre, wg, wu, wd, post):
    n, d = h2.shape
    t = TOKEN_TILE
    tok = pl.BlockSpec((t, d), lambda i: (i, 0))
    return pl.pallas_call(
        _ffn_kernel,
        grid=(n // t,),
        in_specs=[tok, _const_spec((1, d)), _const_spec(wg.shape), _const_spec(wu.shape),
                  _const_spec(wd.shape), _const_spec((1, d))],
        out_specs=tok,
        out_shape=jax.ShapeDtypeStruct((n, d), F32),
        compiler_params=_params(("parallel",)),
        name="ffn",
    )(h2, pre, wg, wu, wd, post)


def _fnet_a_kernel(x_ref, nw_ref, cs_ref, m1_ref, a_ref):
    gd = FOURIER_GROUP_DIM
    for c in range(x_ref.shape[2] // D_MODEL):
        cl = slice(c * D_MODEL, (c + 1) * D_MODEL)
        xn = _rms(x_ref[0, :, cl], nw_ref[...]).astype(BF16)
        yr, yi = [], []
        for g in range(FOURIER_GROUPS):
            y = _dot(xn[:, g * gd:(g + 1) * gd], cs_ref[...])
            yr.append(y[:, :gd])
            yi.append(y[:, gd:])
        stack = jnp.concatenate([jnp.concatenate(yr, axis=1), jnp.concatenate(yi, axis=1)], axis=0)
        a_ref[0, :, cl] = _dot(m1_ref[...], stack.astype(BF16)).astype(BF16)


def _fnet_b_kernel(a_ref, g_ref, f_ref):
    for c in range(g_ref.shape[0]):
        stack = jnp.concatenate([a_ref[0, 0, c], a_ref[0, 1, c]], axis=0)
        f_ref[0, :, c * D_MODEL:(c + 1) * D_MODEL] = _dot(g_ref[c], stack).astype(BF16)


def _fnet(h, nw, cs, m1, gt):
    b, s, d = h.shape
    n1 = m1.shape[0] // 2
    n2 = s // n1
    ca = min(FNET_COLS, n2)
    cb = min(FNET_COLS, n1)
    a = pl.pallas_call(
        _fnet_a_kernel,
        grid=(b, n2 // ca),
        in_specs=[pl.BlockSpec((1, n1, ca * d), lambda i, j: (i, 0, j)),
                  _const_spec((1, d)), _const_spec(cs.shape), _const_spec(m1.shape)],
        out_specs=pl.BlockSpec((1, 2 * n1, ca * d), lambda i, j: (i, 0, j)),
        out_shape=jax.ShapeDtypeStruct((b, 2 * n1, n2 * d), BF16),
        compiler_params=_params(("parallel", "parallel")),
        name="fnet_a",
    )(h.reshape(b, n1, n2 * d), nw, cs, m1)
    f = pl.pallas_call(
        _fnet_b_kernel,
        grid=(b, n1 // cb),
        in_specs=[pl.BlockSpec((1, 2, cb, n2, d), lambda i, j: (i, 0, j, 0, 0)),
                  pl.BlockSpec((cb, n2, 2 * n2), lambda i, j: (j, 0, 0))],
        out_specs=pl.BlockSpec((1, n2, cb * d), lambda i, j: (i, 0, j)),
        out_shape=jax.ShapeDtypeStruct((b, n2, n1 * d), BF16),
        compiler_params=_params(("parallel", "parallel")),
        name="fnet_b",
    )(a.reshape(b, 2, n1, n2, d), gt)
    return f.reshape(b, s, d)


def _rope_tables(s):
    inv = ROPE_THETA ** (-jnp.arange(0, QK_ROPE, 2, dtype=F32) / QK_ROPE)
    ang = jnp.arange(s, dtype=F32)[:, None] * inv[None, :]
    cos2 = jnp.concatenate([jnp.cos(ang), jnp.cos(ang)], axis=1)
    sin2 = jnp.concatenate([jnp.sin(ang), jnp.sin(ang)], axis=1)
    pad = HEAD_PAD - QK_NOPE - QK_ROPE
    cos_t = jnp.concatenate([jnp.ones((s, QK_NOPE), F32), cos2, jnp.ones((s, pad), F32)], axis=1)
    sin_t = jnp.concatenate([jnp.zeros((s, QK_NOPE), F32), sin2, jnp.zeros((s, pad), F32)], axis=1)
    return cos_t, sin_t


def _rot_cols(w):
    half = w.shape[-1] // 2
    return jnp.concatenate([-w[..., half:], w[..., :half]], axis=-1)


def _pad_cols(w, left, total):
    return jnp.pad(w, ((0, 0), (left, total - left - w.shape[1])))


def _even_weights(w_in, w_uq, w_ukv):
    o1 = SSM_D
    o2 = o1 + CONV_CH
    o3 = o2 + 2 * SSM_HEADS
    o4 = o3 + Q_LORA
    o5 = o4 + KV_LORA
    w_z, w_xbc, w_dt, w_cq, w_ckv, w_kr = (w_in[:, :o1], w_in[:, o1:o2], w_in[:, o2:o3],
                                             w_in[:, o3:o4], w_in[:, o4:o5], w_in[:, o5:])
    win = jnp.concatenate([
        w_z, w_xbc, w_cq, w_ckv,
        _pad_cols(w_kr, QK_NOPE, LANE), _pad_cols(_rot_cols(w_kr), QK_NOPE, LANE),
        _pad_cols(w_dt, 0, LANE)], axis=1).astype(BF16)
    dq = QK_NOPE + QK_ROPE
    wq = w_uq.reshape(Q_LORA, MLA_HEADS, dq)
    zq = jnp.zeros((Q_LORA, MLA_HEADS, HEAD_PAD - dq), F32)
    wqm = jnp.concatenate([wq, zq], axis=-1).reshape(Q_LORA, -1).astype(BF16)
    wqr = jnp.concatenate([jnp.zeros((Q_LORA, MLA_HEADS, QK_NOPE), F32),
                           _rot_cols(wq[..., QK_NOPE:]), zq], axis=-1).reshape(Q_LORA, -1).astype(BF16)
    wkv = w_ukv.reshape(KV_LORA, MLA_HEADS, QK_NOPE + V_DIM)
    zk = jnp.zeros((KV_LORA, MLA_HEADS, HEAD_PAD - QK_NOPE), F32)
    wk = jnp.concatenate([wkv[..., :QK_NOPE], zk], axis=-1).reshape(KV_LORA, -1).astype(BF16)
    zv = jnp.zeros((KV_LORA, MLA_HEADS, HEAD_PAD - V_DIM), F32)
    wv = jnp.concatenate([wkv[..., QK_NOPE:], zv], axis=-1).reshape(KV_LORA, -1).astype(BF16)
    return win, wqm, wqr, wk, wv


def _fnet_tables(s):
    n2 = CHUNK
    n1 = s // n2
    gd = FOURIER_GROUP_DIM
    ci = jnp.arange(gd, dtype=jnp.int32)
    ang_c = (2.0 * math.pi / gd) * ((ci[:, None] * ci[None, :]) % gd).astype(F32)
    cs = (jnp.concatenate([jnp.cos(ang_c), -jnp.sin(ang_c)], axis=1) * gd ** -0.5).astype(BF16)
    i1 = jnp.arange(n1, dtype=jnp.int32)
    ang1 = (2.0 * math.pi / n1) * ((i1[:, None] * i1[None, :]) % n1).astype(F32)
    c1, s1 = jnp.cos(ang1), jnp.sin(ang1)
    m1 = jnp.concatenate([jnp.concatenate([c1, s1], axis=1),
                          jnp.concatenate([-s1, c1], axis=1)], axis=0).astype(BF16)
    k1 = jnp.arange(n1, dtype=jnp.int32)[:, None, None]
    k2 = jnp.arange(n2, dtype=jnp.int32)[None, :, None]
    j2 = jnp.arange(n2, dtype=jnp.int32)[None, None, :]
    ang = (2.0 * math.pi / s) * ((j2 * (k1 + n1 * k2)) % s).astype(F32)
    gt = (jnp.concatenate([jnp.cos(ang), jnp.sin(ang)], axis=2) * s ** -0.5).astype(BF16)
    return cs, m1, gt


def _row(v, width=None):
    v = v.astype(F32).reshape(1, -1)
    if width is not None:
        v = jnp.pad(v, ((0, 0), (0, width - v.shape[1])))
    return v


def _trunk(x, kv_layers, kv_off, p):
    b, s, d = x.shape
    n = b * s
    h = x
    z, xbc, dt, q, k, v = _inproj(h.reshape(n, d), p["mix_pre"][0], p["win"], p["q_norm"], p["kv_norm"],
                                  p["wqm"], p["wqr"], p["wk"], p["wv"], p["cos"][:s], p["sin"][:s],
                                  p["conv_w"], p["conv_b"], s)
    yf, yb = _ssd(xbc.reshape(b, s, -1), dt.reshape(b, s, -1), p["dt_bias"], p["a_row"], p["d_skip"])
    hp = MLA_HEADS * HEAD_PAD
    o = _flash(q.reshape(b, s, hp), k.reshape(b, s, hp), v.reshape(b, s, hp))
    h = _postmix(_postmix_even_kernel, h, (yf, yb, z.reshape(b, s, -1), o),
                 (SSM_D, SSM_D, SSM_D, MLA_D),
                 (p["ssm_norm"], p["w_out"], p["mix_post"][0]), kv_layers[0], kv_off,
                 (p["xa_pre"][0], p["xa_wq"][0], p["xa_wo"][0], p["xa_post"][0]))
    h = _ffn(h.reshape(n, d), p["ffn_pre"][0], p["wg"][0], p["wu"][0], p["wd"][0],
             p["ffn_post"][0]).reshape(b, s, d)
    cs, m1, gt = _fnet_tables(s)
    f = _fnet(h, p["mix_pre"][1], cs, m1, gt)
    h = _postmix(_postmix_odd_kernel, h, (f,), (d,), (p["w_mix"], p["mix_post"][1]),
                 kv_layers[1], kv_off,
                 (p["xa_pre"][1], p["xa_wq"][1], p["xa_wo"][1], p["xa_post"][1]))
    h = _ffn(h.reshape(n, d), p["ffn_pre"][1], p["wg"][1], p["wu"][1], p["wd"][1],
             p["ffn_post"][1]).reshape(b, s, d)
    return h


def kernel(x_prompt, x_sample, mem_prompt, mem_sample, norm_mix_pre, norm_mix_post, norm_xa_pre, norm_xa_post, norm_mem, xa_wq, xa_wkv, xa_wo, norm_ffn_pre, norm_ffn_post, ffn_w_gu, ffn_w_down, ev_w_in, ev_conv_w, ev_conv_b, ev_a_log_f, ev_a_log_b, ev_dt_bias_f, ev_dt_bias_b, ev_d_skip, ev_ssm_norm, ev_q_norm, ev_w_uq, ev_kv_norm, ev_w_ukv, ev_w_out, od_w_mix):
    depth = norm_mix_pre.shape[0]
    d_ff = ffn_w_down.shape[1]
    s_max = max(x_prompt.shape[1], x_sample.shape[1])
    cos_t, sin_t = _rope_tables(s_max)
    win, wqm, wqr, wk, wv = _even_weights(ev_w_in[0], ev_w_uq[0], ev_w_ukv[0])
    rows = lambda w: [_row(w[i]) for i in range(depth)]
    p = {
        "mix_pre": rows(norm_mix_pre), "mix_post": rows(norm_mix_post),
        "xa_pre": rows(norm_xa_pre), "xa_post": rows(norm_xa_post),
        "ffn_pre": rows(norm_ffn_pre), "ffn_post": rows(norm_ffn_post),
        "xa_wq": [xa_wq[i].astype(BF16) for i in range(depth)],
        "xa_wo": [xa_wo[i].astype(BF16) for i in range(depth)],
        "wg": [ffn_w_gu[i, :, :d_ff].astype(BF16) for i in range(depth)],
        "wu": [ffn_w_gu[i, :, d_ff:].astype(BF16) for i in range(depth)],
        "wd": [ffn_w_down[i].astype(BF16) for i in range(depth)],
        "win": win, "wqm": wqm, "wqr": wqr, "wk": wk, "wv": wv,
        "q_norm": _row(ev_q_norm[0]), "kv_norm": _row(ev_kv_norm[0]),
        "cos": cos_t, "sin": sin_t,
        "conv_w": jnp.pad(ev_conv_w[0].astype(F32), ((0, SUBLANE - D_CONV), (0, 0))),
        "conv_b": _row(ev_conv_b[0]),
        "dt_bias": _row(jnp.concatenate([ev_dt_bias_f[0], ev_dt_bias_b[0]]), LANE),
        "a_row": _row(-jnp.exp(jnp.concatenate([ev_a_log_f[0], ev_a_log_b[0]]).astype(F32)), LANE),
        "d_skip": _row(jnp.repeat(ev_d_skip[0].astype(F32), SSM_HEAD_DIM)),
        "ssm_norm": _row(ev_ssm_norm[0]),
        "w_out": ev_w_out[0].astype(BF16),
        "w_mix": od_w_mix[0].astype(BF16),
    }
    mem = jnp.concatenate([mem_prompt, mem_sample], axis=0)
    kv_layers = [_memkv(mem, _row(norm_mem[i]), xa_wkv[i].astype(BF16)) for i in range(depth)]
    y_prompt = _trunk(x_prompt, kv_layers, 0, p)
    y_sample = _trunk(x_sample, kv_layers, x_prompt.shape[0], p)
    return (y_prompt, y_sample)
```

```python
import functools
import math

import jax
import jax.numpy as jnp
from jax import lax
from jax.experimental import pallas as pl
from jax.experimental.pallas import tpu as pltpu

F32 = jnp.float32
BF16 = jnp.bfloat16

EPS = 1e-6
D_MODEL = 1024
N_MEM = 256

SSM_HEADS = 8
SSM_HEAD_DIM = 64
SSM_D = SSM_HEADS * SSM_HEAD_DIM
SSM_GROUPS = 2
SSM_HPG = SSM_HEADS // SSM_GROUPS
SSM_STATE = 128
D_CONV = 5
CONV_CH = SSM_D + 2 * SSM_GROUPS * SSM_STATE
CHUNK = 128

MLA_HEADS = 8
QK_NOPE = 64
QK_ROPE = 32
V_DIM = 64
Q_LORA = 256
KV_LORA = 128
ROPE_THETA = 10000.0
MLA_D = MLA_HEADS * V_DIM

FOURIER_GROUPS = 4
FOURIER_GROUP_DIM = D_MODEL // FOURIER_GROUPS

XA_HEADS = 4
XA_HEAD_DIM = D_MODEL // XA_HEADS

LANE = 128
SUBLANE = 8
HEAD_PAD = LANE
V_ONE_LANE = V_DIM
VMEM_LIMIT = 56 * 1024 * 1024

TOKEN_TILE = 512
FFN_CHUNK = 256
ATTN_TQ = 1024
ATTN_TK = 2048
ATTN_UNROLL = 2
FNET_COLS = 8
SSD_CHUNKS_PER_STEP = 2
LOG2E = 1.4426950408889634

O_Z = 0
O_XBC = O_Z + SSM_D
O_CQ = O_XBC + CONV_CH
O_CKV = O_CQ + Q_LORA
O_KA = O_CKV + KV_LORA
O_KB = O_KA + LANE
O_DT = O_KB + LANE
D_IN_PAD = O_DT + LANE


def _params(sem, vmem=VMEM_LIMIT):
    return pltpu.CompilerParams(dimension_semantics=sem, vmem_limit_bytes=vmem)


def _rms(x, w):
    ms = jnp.mean(x * x, axis=-1, keepdims=True)
    return x * lax.rsqrt(ms + EPS) * w


def _silu(x):
    return x / (1.0 + jnp.exp(-x))


def _dot(a, b):
    return jnp.dot(a, b, preferred_element_type=F32)


def _dot_nt(a, b):
    return lax.dot_general(a, b, (((1,), (1,)), ((), ())), preferred_element_type=F32)


def _const_spec(shape):
    nd = len(shape)
    return pl.BlockSpec(shape, lambda *_: (0,) * nd)


def _memkv_kernel(mem_ref, nw_ref, wkv_ref, kv_ref):
    xn = _rms(mem_ref[0], nw_ref[...]).astype(BF16)
    kv_ref[0] = _dot(xn, wkv_ref[...]).astype(BF16)


def _memkv(mem, nw, wkv):
    b, m, d = mem.shape
    return pl.pallas_call(
        _memkv_kernel,
        grid=(b,),
        in_specs=[pl.BlockSpec((1, m, d), lambda i: (i, 0, 0)),
                  _const_spec((1, d)),
                  _const_spec((d, 2 * d))],
        out_specs=pl.BlockSpec((1, m, 2 * d), lambda i: (i, 0, 0)),
        out_shape=jax.ShapeDtypeStruct((b, m, 2 * d), BF16),
        compiler_params=_params(("parallel",)),
        name="memkv",
    )(mem, nw, wkv)


def _inproj_kernel(xp_ref, x_ref, xn_ref, nw_ref, win_ref, qnw_ref, kvnw_ref, wqm_ref, wqr_ref,
                   wk_ref, wv_ref, cos_ref, sin_ref, cw_ref, cb_ref,
                   z_ref, xbc_ref, dt_ref, q_ref, k_ref, v_ref, *, tiles_per_seq):
    t = x_ref.shape[0]
    pos = pl.program_id(0) % tiles_per_seq
    x_ext = jnp.concatenate([xp_ref[...], x_ref[...], xn_ref[...]], axis=0)
    hn = _rms(x_ext, nw_ref[...]).astype(BF16)
    proj_ext = _dot(hn, win_ref[...])
    proj = proj_ext[SUBLANE:SUBLANE + t, :]
    z_ref[...] = proj[:, O_Z:O_XBC]
    dt_ref[...] = proj[:, O_DT:D_IN_PAD]
    row = lax.broadcasted_iota(jnp.int32, (t + 2 * SUBLANE, 1), 0)
    inside = ((row >= SUBLANE) | (pos > 0)) & ((row < SUBLANE + t) | (pos < tiles_per_seq - 1))
    xbc_ext = jnp.where(inside, proj_ext[:, O_XBC:O_CQ], 0.0)
    acc = cb_ref[...] + jnp.zeros((t, CONV_CH), F32)
    n_ext = t + 2 * SUBLANE
    for j in range(D_CONV):
        tap = pltpu.roll(xbc_ext, (D_CONV // 2 - j) % n_ext, 0)[SUBLANE:SUBLANE + t, :]
        acc = acc + tap * cw_ref[j:j + 1, :]
    xbc_ref[...] = _silu(acc).astype(BF16)
    cqn = _rms(proj[:, O_CQ:O_CKV], qnw_ref[...]).astype(BF16)
    ckvn = _rms(proj[:, O_CKV:O_KA], kvnw_ref[...]).astype(BF16)
    cos_t = cos_ref[...]
    sin_t = sin_ref[...]
    qscale = LOG2E * (QK_NOPE + QK_ROPE) ** -0.5
    cos_q = cos_t * qscale
    sin_q = sin_t * qscale
    qm = _dot(cqn, wqm_ref[...])
    qr = _dot(cqn, wqr_ref[...])
    kr = proj[:, O_KA:O_KB] * cos_t + proj[:, O_KB:O_DT] * sin_t
    km = _dot(ckvn, wk_ref[...])
    vm = _dot(ckvn, wv_ref[...])
    lane = lax.broadcasted_iota(jnp.int32, (1, HEAD_PAD), 1)
    one_col = jnp.where(lane == V_ONE_LANE, 1.0, 0.0).astype(F32)
    for h in range(MLA_HEADS):
        sl = slice(h * HEAD_PAD, (h + 1) * HEAD_PAD)
        q_ref[:, sl] = (qm[:, sl] * cos_q + qr[:, sl] * sin_q).astype(BF16)
        k_ref[:, sl] = (km[:, sl] + kr).astype(BF16)
        v_ref[:, sl] = (vm[:, sl] + one_col).astype(BF16)


def _inproj(x2, nw, win, qnw, kvnw, wqm, wqr, wk, wv, cos_t, sin_t, cw, cb, seq):
    n, d = x2.shape
    t = TOKEN_TILE
    tiles_per_seq = seq // t
    rb = t // SUBLANE
    nrb = n // SUBLANE
    tok = lambda w: pl.BlockSpec((t, w), lambda i: (i, 0))
    halo_prev = pl.BlockSpec((SUBLANE, d), lambda i: (jnp.maximum(i * rb - 1, 0), 0))
    halo_next = pl.BlockSpec((SUBLANE, d), lambda i: (jnp.minimum(i * rb + rb, nrb - 1), 0))
    pos = pl.BlockSpec((t, HEAD_PAD), lambda i: (i % tiles_per_seq, 0))
    hp = MLA_HEADS * HEAD_PAD
    outs = [jax.ShapeDtypeStruct((n, SSM_D), F32), jax.ShapeDtypeStruct((n, CONV_CH), BF16),
            jax.ShapeDtypeStruct((n, LANE), F32), jax.ShapeDtypeStruct((n, hp), BF16),
            jax.ShapeDtypeStruct((n, hp), BF16), jax.ShapeDtypeStruct((n, hp), BF16)]
    return pl.pallas_call(
        functools.partial(_inproj_kernel, tiles_per_seq=tiles_per_seq),
        grid=(n // t,),
        in_specs=[halo_prev, tok(d), halo_next, _const_spec((1, d)), _const_spec((d, D_IN_PAD)),
                  _const_spec((1, Q_LORA)), _const_spec((1, KV_LORA)),
                  _const_spec((Q_LORA, hp)), _const_spec((Q_LORA, hp)),
                  _const_spec((KV_LORA, hp)), _const_spec((KV_LORA, hp)),
                  pos, pos, _const_spec((SUBLANE, CONV_CH)), _const_spec((1, CONV_CH))],
        out_specs=[tok(SSM_D), tok(CONV_CH), tok(LANE), tok(hp), tok(hp), tok(hp)],
        out_shape=outs,
        compiler_params=_params(("parallel",)),
        name="inproj",
    )(x2, x2, x2, nw, win, qnw, kvnw, wqm, wqr, wk, wv, cos_t, sin_t, cw, cb)


def _softplus(x):
    return jnp.maximum(x, 0.0) + jnp.log1p(jnp.exp(-jnp.abs(x)))


def _dot_pieces(a, passes, fn):
    out = None
    rem = a
    for _ in range(passes):
        piece = rem.astype(BF16)
        rem = rem - piece.astype(F32)
        term = fn(piece)
        out = term if out is None else out + term
    return out


def _ssd_chunk(xbc, dt_raw, dtb_ref, a_ref, e64_ref, h_in, reverse, lane_off):
    L = CHUNK
    P = SSM_HEAD_DIM
    gw = SSM_HPG * P
    xs = xbc[:, :SSM_D].astype(F32)
    bm = xbc[:, SSM_D:SSM_D + SSM_GROUPS * SSM_STATE]
    cm = xbc[:, SSM_D + SSM_GROUPS * SSM_STATE:]
    dt = _softplus(dt_raw + dtb_ref[...])
    dta = dt * a_ref[...]
    row = lax.broadcasted_iota(jnp.int32, (L, L), 0)
    col = lax.broadcasted_iota(jnp.int32, (L, L), 1)
    mask = (col >= row) if reverse else (col <= row)
    tri = jnp.where(mask, 1.0, 0.0).astype(BF16)
    cum = _dot_pieces(dta, 3, lambda piece: _dot(tri, piece))
    cum_t = cum.T
    dt_t = dt.T
    last = 0 if reverse else L - 1
    exp_cum = jnp.exp(cum)
    w_state = dt * jnp.exp(cum[last:last + 1, :] - cum)
    scal64 = _dot_pieces(jnp.concatenate([w_state, exp_cum], axis=0), 2,
                         lambda piece: _dot(piece, e64_ref[...]))
    xw = (xs * scal64[:L]).astype(BF16)
    exp_cum64 = scal64[L:]
    exp_total64 = exp_cum64[last:last + 1, :]
    lane = lax.broadcasted_iota(jnp.int32, (L, 2 * P), 1)
    ys = []
    h_out = []
    for g in range(SSM_GROUPS):
        bg = bm[:, g * SSM_STATE:(g + 1) * SSM_STATE]
        cg = cm[:, g * SSM_STATE:(g + 1) * SSM_STATE]
        cb = _dot_nt(cg, bg)
        h_prev = h_in[g]
        y_off = _dot(cg, h_prev.astype(BF16)) * exp_cum64[:, g * gw:(g + 1) * gw]
        pairs = []
        for pr in range(SSM_HPG // 2):
            x_pair = xbc[:, g * gw + pr * 2 * P:g * gw + (pr + 1) * 2 * P]
            halves = []
            for q in range(2):
                hh = g * SSM_HPG + pr * 2 + q
                ln = lane_off + hh
                seg = cum[:, ln:ln + 1] - cum_t[ln:ln + 1, :]
                dec = jnp.exp(jnp.where(mask, seg, -jnp.inf))
                m = (cb * dec * dt_t[ln:ln + 1, :]).astype(BF16)
                halves.append(_dot(m, x_pair))
            pairs.append(jnp.where(lane < P, halves[0], halves[1]))
        ys.append(y_off + jnp.concatenate(pairs, axis=1))
        st = _dot(bg.astype(F32).T.astype(BF16), xw[:, g * gw:(g + 1) * gw])
        h_out.append(h_prev * exp_total64[:, g * gw:(g + 1) * gw] + st)
    return jnp.concatenate(ys, axis=1), xs, h_out


def _ssd_kernel(xf_ref, xb_ref, dtf_ref, dtb_ref, bias_ref, a_ref, dskip_ref,
                e64f_ref, e64b_ref, yf_ref, yb_ref, hf_ref, hb_ref):
    @pl.when(pl.program_id(1) == 0)
    def _():
        hf_ref[...] = jnp.zeros_like(hf_ref)
        hb_ref[...] = jnp.zeros_like(hb_ref)

    n_sub = xf_ref.shape[1] // CHUNK
    h_f = [hf_ref[g] for g in range(SSM_GROUPS)]
    h_b = [hb_ref[g] for g in range(SSM_GROUPS)]
    for i in range(n_sub):
        rows = slice(i * CHUNK, (i + 1) * CHUNK)
        y_f, xs_f, h_f = _ssd_chunk(xf_ref[0, rows, :], dtf_ref[0, rows, :], bias_ref, a_ref,
                                    e64f_ref, h_f, False, 0)
        yf_ref[0, rows, :] = y_f + xs_f * dskip_ref[...]
        rows = slice((n_sub - 1 - i) * CHUNK, (n_sub - i) * CHUNK)
        y_b, _, h_b = _ssd_chunk(xb_ref[0, rows, :], dtb_ref[0, rows, :], bias_ref, a_ref,
                                 e64b_ref, h_b, True, SSM_HEADS)
        yb_ref[0, rows, :] = y_b
    for g in range(SSM_GROUPS):
        hf_ref[g] = h_f[g]
        hb_ref[g] = h_b[g]


def _head_lane_expander(lane_off, width):
    src_lane = lax.broadcasted_iota(jnp.int32, (LANE, SSM_HEADS * width), 0)
    dst_head = lax.broadcasted_iota(jnp.int32, (LANE, SSM_HEADS * width), 1) // width
    return (src_lane == dst_head + lane_off).astype(BF16)


def _ssd(xbc, dt, bias, a_row, dskip):
    b, s, _ = xbc.shape
    expanders = [_head_lane_expander(off, SSM_HEAD_DIM) for off in (0, SSM_HEADS)]
    rows = SSD_CHUNKS_PER_STEP * CHUNK
    nc = s // rows
    chunk = lambda w, f: pl.BlockSpec((1, rows, w), lambda i, c: (i, f(c), 0))
    fwd = lambda c: c
    bwd = lambda c: nc - 1 - c
    hshape = (SSM_GROUPS, SSM_STATE, SSM_HPG * SSM_HEAD_DIM)
    return pl.pallas_call(
        _ssd_kernel,
        grid=(b, nc),
        in_specs=[chunk(CONV_CH, fwd), chunk(CONV_CH, bwd), chunk(LANE, fwd), chunk(LANE, bwd),
                  _const_spec((1, LANE)), _const_spec((1, LANE)), _const_spec((1, SSM_D))]
        + [_const_spec(e.shape) for e in expanders],
        out_specs=[chunk(SSM_D, fwd), chunk(SSM_D, bwd)],
        out_shape=[jax.ShapeDtypeStruct((b, s, SSM_D), F32)] * 2,
        scratch_shapes=[pltpu.VMEM(hshape, F32), pltpu.VMEM(hshape, F32)],
        compiler_params=_params(("parallel", "arbitrary")),
        name="ssd",
    )(xbc, xbc, dt, dt, bias, a_row, dskip, *expanders)


def _flash_kernel(q_ref, k_ref, v_ref, o_ref, *, tk):
    s_len = k_ref.shape[1]
    tq = q_ref.shape[1]
    sls = [slice(hh * HEAD_PAD, (hh + 1) * HEAD_PAD) for hh in range(2)]

    def body(j, carry):
        off = pl.multiple_of(j * tk, tk)
        new = []
        for sl, (m, acc) in zip(sls, carry):
            kj = k_ref[0, pl.ds(off, tk), sl]
            vj = v_ref[0, pl.ds(off, tk), sl]
            s = _dot_nt(q_ref[0, :, sl], kj)
            m_new = jnp.maximum(m, jnp.max(s, axis=1, keepdims=True))
            alpha = jnp.exp2(m - m_new)
            p = jnp.exp2(s - m_new).astype(BF16)
            new.append((m_new, acc * alpha + _dot(p, vj)))
        return tuple(new)

    m0 = jnp.full((tq, 1), -jnp.inf, F32)
    acc0 = jnp.zeros((tq, HEAD_PAD), F32)
    carry = lax.fori_loop(0, s_len // tk, body, ((m0, acc0), (m0, acc0)), unroll=ATTN_UNROLL)
    outs = [acc / acc[:, V_ONE_LANE:V_ONE_LANE + 1] for _, acc in carry]
    lane = lax.broadcasted_iota(jnp.int32, (tq, HEAD_PAD), 1)
    o_ref[0] = jnp.where(lane < V_DIM, outs[0], pltpu.roll(outs[1], V_DIM, 1)).astype(BF16)


def _flash(q, k, v):
    b, s, hp = q.shape
    tq = min(ATTN_TQ, s)
    tk = min(ATTN_TK, s)
    pairs = MLA_HEADS // 2
    pw = 2 * HEAD_PAD
    return pl.pallas_call(
        functools.partial(_flash_kernel, tk=tk),
        grid=(b, pairs, s // tq),
        in_specs=[pl.BlockSpec((1, tq, pw), lambda i, p, j: (i, j, p)),
                  pl.BlockSpec((1, s, pw), lambda i, p, j: (i, 0, p)),
                  pl.BlockSpec((1, s, pw), lambda i, p, j: (i, 0, p))],
        out_specs=pl.BlockSpec((1, tq, 2 * V_DIM), lambda i, p, j: (i, j, p)),
        out_shape=jax.ShapeDtypeStruct((b, s, MLA_D), BF16),
        compiler_params=_params(("parallel", "parallel", "arbitrary")),
        name="mla_flash",
    )(q, k, v)


def _cross_attn(h1, kv_ref, pre_w, wq_ref, wo_ref, post_w):
    hn = _rms(h1, pre_w).astype(BF16)
    q = (_dot(hn, wq_ref[...]) * (XA_HEAD_DIM ** -0.5)).astype(BF16)
    heads = []
    for hd in range(XA_HEADS):
        sl = slice(hd * XA_HEAD_DIM, (hd + 1) * XA_HEAD_DIM)
        kh = kv_ref[0, :, sl]
        vh = kv_ref[0, :, D_MODEL + hd * XA_HEAD_DIM:D_MODEL + (hd + 1) * XA_HEAD_DIM]
        s = _dot_nt(q[:, sl], kh)
        p = jnp.exp(s - jnp.max(s, axis=1, keepdims=True))
        l = jnp.sum(p, axis=1, keepdims=True)
        heads.append((_dot(p.astype(BF16), vh) / l).astype(BF16))
    o = jnp.concatenate(heads, axis=1)
    xa = _dot(o, wo_ref[...])
    return h1 + _rms(xa, post_w)


def _postmix_even_kernel(h_ref, yf_ref, yb_ref, z_ref, o_ref, snw_ref, wout_ref, mpost_ref,
                         kv_ref, xpre_ref, wq_ref, wo_ref, xpost_ref, out_ref):
    y = (yf_ref[0] + yb_ref[0]) * _silu(z_ref[0])
    gw = SSM_D // SSM_GROUPS
    parts = []
    for g in range(SSM_GROUPS):
        yg = y[:, g * gw:(g + 1) * gw]
        parts.append(yg * lax.rsqrt(jnp.mean(yg * yg, axis=-1, keepdims=True) + EPS))
    y_ssd = (jnp.concatenate(parts, axis=1) * snw_ref[...]).astype(BF16)
    mix = _dot(y_ssd, wout_ref[:SSM_D, :]) + _dot(o_ref[0], wout_ref[SSM_D:, :])
    h1 = h_ref[0] + _rms(mix, mpost_ref[...])
    out_ref[0] = _cross_attn(h1, kv_ref, xpre_ref[...], wq_ref, wo_ref, xpost_ref[...])


def _postmix_odd_kernel(h_ref, f_ref, wmix_ref, mpost_ref,
                        kv_ref, xpre_ref, wq_ref, wo_ref, xpost_ref, out_ref):
    mix = _dot(f_ref[0], wmix_ref[...])
    h1 = h_ref[0] + _rms(mix, mpost_ref[...])
    out_ref[0] = _cross_attn(h1, kv_ref, xpre_ref[...], wq_ref, wo_ref, xpost_ref[...])


def _postmix(kernel, h, mixed, mixed_w, consts_a, kv, kv_off, consts_b):
    b, s, d = h.shape
    t = TOKEN_TILE
    tok = lambda w: pl.BlockSpec((1, t, w), lambda i, j: (i, j, 0))
    in_specs = [tok(d)] + [tok(w) for w in mixed_w]
    in_specs += [_const_spec(c.shape) for c in consts_a]
    in_specs += [pl.BlockSpec((1, N_MEM, 2 * d), lambda i, j: (i + kv_off, 0, 0))]
    in_specs += [_const_spec(c.shape) for c in consts_b]
    return pl.pallas_call(
        kernel,
        grid=(b, s // t),
        in_specs=in_specs,
        out_specs=tok(d),
        out_shape=jax.ShapeDtypeStruct((b, s, d), F32),
        compiler_params=_params(("parallel", "parallel")),
        name="postmix",
    )(h, *mixed, *consts_a, kv, *consts_b)


def _ffn_kernel(h_ref, pre_ref, wg_ref, wu_ref, wd_ref, post_ref, *rest):
    out_ref = rest[-2] if len(rest) == 3 else rest[0]
    h = h_ref[...]
    hn = _rms(h, pre_ref[...]).astype(BF16)
    d_ff = wg_ref.shape[1]
    acc = jnp.zeros(h.shape, F32)
    for c in range(d_ff // FFN_CHUNK):
        sl = slice(c * FFN_CHUNK, (c + 1) * FFN_CHUNK)
        g = _dot(hn, wg_ref[:, sl])
        u = _dot(hn, wu_ref[:, sl])
        acc = acc + _dot((_silu(g) * u).astype(BF16), wd_ref[sl, :])
    out = h + _rms(acc, post_ref[...])
    out_ref[...] = out
    if len(rest) == 3:
        rest[2][...] = _rms(out, rest[0][...]).astype(BF16)


def _ffn(h2, pre, wg, wu, wd, post, next_pre=None):
    n, d = h2.shape
    t = TOKEN_TILE
    tok = pl.BlockSpec((t, d), lambda i: (i, 0))
    in_specs = [tok, _const_spec((1, d)), _const_spec(wg.shape), _const_spec(wu.shape),
                _const_spec(wd.shape), _const_spec((1, d))]
    args = [h2, pre, wg, wu, wd, post]
    out_specs, out_shape = tok, jax.ShapeDtypeStruct((n, d), F32)
    if next_pre is not None:
        in_specs.append(_const_spec((1, d)))
        args.append(next_pre)
        out_specs, out_shape = [tok, tok], [out_shape, jax.ShapeDtypeStruct((n, d), BF16)]
    return pl.pallas_call(
        _ffn_kernel,
        grid=(n // t,),
        in_specs=in_specs,
        out_specs=out_specs,
        out_shape=out_shape,
        compiler_params=_params(("parallel",)),
        name="ffn",
    )(*args)


def _fnet_a_kernel(x_ref, cs_ref, m1_ref, a_ref):
    gd = FOURIER_GROUP_DIM
    for c in range(x_ref.shape[2] // D_MODEL):
        cl = slice(c * D_MODEL, (c + 1) * D_MODEL)
        xn = x_ref[0, :, cl]
        yr, yi = [], []
        for g in range(FOURIER_GROUPS):
            y = _dot(xn[:, g * gd:(g + 1) * gd], cs_ref[...])
            yr.append(y[:, :gd])
            yi.append(y[:, gd:])
        stack = jnp.concatenate([jnp.concatenate(yr, axis=1), jnp.concatenate(yi, axis=1)], axis=0)
        a_ref[0, :, cl] = _dot(m1_ref[...], stack.astype(BF16)).astype(BF16)


def _fnet_b_kernel(a_ref, g_ref, f_ref):
    for c in range(g_ref.shape[0]):
        stack = jnp.concatenate([a_ref[0, 0, c], a_ref[0, 1, c]], axis=0)
        f_ref[0, :, c * D_MODEL:(c + 1) * D_MODEL] = _dot(g_ref[c], stack).astype(BF16)


def _fnet(xn, cs, m1, gt):
    b, s, d = xn.shape
    n1 = m1.shape[0] // 2
    n2 = s // n1
    ca = min(FNET_COLS, n2)
    cb = min(FNET_COLS, n1)
    a = pl.pallas_call(
        _fnet_a_kernel,
        grid=(b, n2 // ca),
        in_specs=[pl.BlockSpec((1, n1, ca * d), lambda i, j: (i, 0, j)),
                  _const_spec(cs.shape), _const_spec(m1.shape)],
        out_specs=pl.BlockSpec((1, 2 * n1, ca * d), lambda i, j: (i, 0, j)),
        out_shape=jax.ShapeDtypeStruct((b, 2 * n1, n2 * d), BF16),
        compiler_params=_params(("parallel", "parallel")),
        name="fnet_a",
    )(xn.reshape(b, n1, n2 * d), cs, m1)
    f = pl.pallas_call(
        _fnet_b_kernel,
        grid=(b, n1 // cb),
        in_specs=[pl.BlockSpec((1, 2, cb, n2, d), lambda i, j: (i, 0, j, 0, 0)),
                  pl.BlockSpec((cb, n2, 2 * n2), lambda i, j: (j, 0, 0))],
        out_specs=pl.BlockSpec((1, n2, cb * d), lambda i, j: (i, 0, j)),
        out_shape=jax.ShapeDtypeStruct((b, n2, n1 * d), BF16),
        compiler_params=_params(("parallel", "parallel")),
        name="fnet_b",
    )(a.reshape(b, 2, n1, n2, d), gt)
    return f.reshape(b, s, d)


def _rope_tables(s):
    inv = ROPE_THETA ** (-jnp.arange(0, QK_ROPE, 2, dtype=F32) / QK_ROPE)
    ang = jnp.arange(s, dtype=F32)[:, None] * inv[None, :]
    cos2 = jnp.concatenate([jnp.cos(ang), jnp.cos(ang)], axis=1)
    sin2 = jnp.concatenate([jnp.sin(ang), jnp.sin(ang)], axis=1)
    pad = HEAD_PAD - QK_NOPE - QK_ROPE
    cos_t = jnp.concatenate([jnp.ones((s, QK_NOPE), F32), cos2, jnp.ones((s, pad), F32)], axis=1)
    sin_t = jnp.concatenate([jnp.zeros((s, QK_NOPE), F32), sin2, jnp.zeros((s, pad), F32)], axis=1)
    return cos_t, sin_t


def _rot_cols(w):
    half = w.shape[-1] // 2
    return jnp.concatenate([-w[..., half:], w[..., :half]], axis=-1)


def _pad_cols(w, left, total):
    return jnp.pad(w, ((0, 0), (left, total - left - w.shape[1])))


def _even_weights(w_in, w_uq, w_ukv):
    o1 = SSM_D
    o2 = o1 + CONV_CH
    o3 = o2 + 2 * SSM_HEADS
    o4 = o3 + Q_LORA
    o5 = o4 + KV_LORA
    w_z, w_xbc, w_dt, w_cq, w_ckv, w_kr = (w_in[:, :o1], w_in[:, o1:o2], w_in[:, o2:o3],
                                             w_in[:, o3:o4], w_in[:, o4:o5], w_in[:, o5:])
    win = jnp.concatenate([
        w_z, w_xbc, w_cq, w_ckv,
        _pad_cols(w_kr, QK_NOPE, LANE), _pad_cols(_rot_cols(w_kr), QK_NOPE, LANE),
        _pad_cols(w_dt, 0, LANE)], axis=1).astype(BF16)
    dq = QK_NOPE + QK_ROPE
    wq = w_uq.reshape(Q_LORA, MLA_HEADS, dq)
    zq = jnp.zeros((Q_LORA, MLA_HEADS, HEAD_PAD - dq), F32)
    wqm = jnp.concatenate([wq, zq], axis=-1).reshape(Q_LORA, -1).astype(BF16)
    wqr = jnp.concatenate([jnp.zeros((Q_LORA, MLA_HEADS, QK_NOPE), F32),
                           _rot_cols(wq[..., QK_NOPE:]), zq], axis=-1).reshape(Q_LORA, -1).astype(BF16)
    wkv = w_ukv.reshape(KV_LORA, MLA_HEADS, QK_NOPE + V_DIM)
    zk = jnp.zeros((KV_LORA, MLA_HEADS, HEAD_PAD - QK_NOPE), F32)
    wk = jnp.concatenate([wkv[..., :QK_NOPE], zk], axis=-1).reshape(KV_LORA, -1).astype(BF16)
    zv = jnp.zeros((KV_LORA, MLA_HEADS, HEAD_PAD - V_DIM), F32)
    wv = jnp.concatenate([wkv[..., QK_NOPE:], zv], axis=-1).reshape(KV_LORA, -1).astype(BF16)
    return win, wqm, wqr, wk, wv


def _fnet_tables(s):
    n2 = CHUNK
    n1 = s // n2
    gd = FOURIER_GROUP_DIM
    ci = jnp.arange(gd, dtype=jnp.int32)
    ang_c = (2.0 * math.pi / gd) * ((ci[:, None] * ci[None, :]) % gd).astype(F32)
    cs = (jnp.concatenate([jnp.cos(ang_c), -jnp.sin(ang_c)], axis=1) * gd ** -0.5).astype(BF16)
    i1 = jnp.arange(n1, dtype=jnp.int32)
    ang1 = (2.0 * math.pi / n1) * ((i1[:, None] * i1[None, :]) % n1).astype(F32)
    c1, s1 = jnp.cos(ang1), jnp.sin(ang1)
    m1 = jnp.concatenate([jnp.concatenate([c1, s1], axis=1),
                          jnp.concatenate([-s1, c1], axis=1)], axis=0).astype(BF16)
    k1 = jnp.arange(n1, dtype=jnp.int32)[:, None, None]
    k2 = jnp.arange(n2, dtype=jnp.int32)[None, :, None]
    j2 = jnp.arange(n2, dtype=jnp.int32)[None, None, :]
    ang = (2.0 * math.pi / s) * ((j2 * (k1 + n1 * k2)) % s).astype(F32)
    gt = (jnp.concatenate([jnp.cos(ang), jnp.sin(ang)], axis=2) * s ** -0.5).astype(BF16)
    return cs, m1, gt


def _row(v, width=None):
    v = v.astype(F32).reshape(1, -1)
    if width is not None:
        v = jnp.pad(v, ((0, 0), (0, width - v.shape[1])))
    return v


def _trunk(x, kv_layers, kv_off, p):
    b, s, d = x.shape
    n = b * s
    h = x
    z, xbc, dt, q, k, v = _inproj(h.reshape(n, d), p["mix_pre"][0], p["win"], p["q_norm"], p["kv_norm"],
                                  p["wqm"], p["wqr"], p["wk"], p["wv"], p["cos"][:s], p["sin"][:s],
                                  p["conv_w"], p["conv_b"], s)
    yf, yb = _ssd(xbc.reshape(b, s, -1), dt.reshape(b, s, -1), p["dt_bias"], p["a_row"], p["d_skip"])
    hp = MLA_HEADS * HEAD_PAD
    o = _flash(q.reshape(b, s, hp), k.reshape(b, s, hp), v.reshape(b, s, hp))
    h = _postmix(_postmix_even_kernel, h, (yf, yb, z.reshape(b, s, -1), o),
                 (SSM_D, SSM_D, SSM_D, MLA_D),
                 (p["ssm_norm"], p["w_out"], p["mix_post"][0]), kv_layers[0], kv_off,
                 (p["xa_pre"][0], p["xa_wq"][0], p["xa_wo"][0], p["xa_post"][0]))
    h, hn = _ffn(h.reshape(n, d), p["ffn_pre"][0], p["wg"][0], p["wu"][0], p["wd"][0],
                 p["ffn_post"][0], next_pre=p["mix_pre"][1])
    h = h.reshape(b, s, d)
    cs, m1, gt = _fnet_tables(s)
    f = _fnet(hn.reshape(b, s, d), cs, m1, gt)
    h = _postmix(_postmix_odd_kernel, h, (f,), (d,), (p["w_mix"], p["mix_post"][1]),
                 kv_layers[1], kv_off,
                 (p["xa_pre"][1], p["xa_wq"][1], p["xa_wo"][1], p["xa_post"][1]))
    h = _ffn(h.reshape(n, d), p["ffn_pre"][1], p["wg"][1], p["wu"][1], p["wd"][1],
             p["ffn_post"][1]).reshape(b, s, d)
    return h


def kernel(x_prompt, x_sample, mem_prompt, mem_sample, norm_mix_pre, norm_mix_post, norm_xa_pre, norm_xa_post, norm_mem, xa_wq, xa_wkv, xa_wo, norm_ffn_pre, norm_ffn_post, ffn_w_gu, ffn_w_down, ev_w_in, ev_conv_w, ev_conv_b, ev_a_log_f, ev_a_log_b, ev_dt_bias_f, ev_dt_bias_b, ev_d_skip, ev_ssm_norm, ev_q_norm, ev_w_uq, ev_kv_norm, ev_w_ukv, ev_w_out, od_w_mix):
    depth = norm_mix_pre.shape[0]
    d_ff = ffn_w_down.shape[1]
    s_max = max(x_prompt.shape[1], x_sample.shape[1])
    cos_t, sin_t = _rope_tables(s_max)
    win, wqm, wqr, wk, wv = _even_weights(ev_w_in[0], ev_w_uq[0], ev_w_ukv[0])
    rows = lambda w: [_row(w[i]) for i in range(depth)]
    p = {
        "mix_pre": rows(norm_mix_pre), "mix_post": rows(norm_mix_post),
        "xa_pre": rows(norm_xa_pre), "xa_post": rows(norm_xa_post),
        "ffn_pre": rows(norm_ffn_pre), "ffn_post": rows(norm_ffn_post),
        "xa_wq": [xa_wq[i].astype(BF16) for i in range(depth)],
        "xa_wo": [xa_wo[i].astype(BF16) for i in range(depth)],
        "wg": [ffn_w_gu[i, :, :d_ff].astype(BF16) for i in range(depth)],
        "wu": [ffn_w_gu[i, :, d_ff:].astype(BF16) for i in range(depth)],
        "wd": [ffn_w_down[i].astype(BF16) for i in range(depth)],
        "win": win, "wqm": wqm, "wqr": wqr, "wk": wk, "wv": wv,
        "q_norm": _row(ev_q_norm[0]), "kv_norm": _row(ev_kv_norm[0]),
        "cos": cos_t, "sin": sin_t,
        "conv_w": jnp.pad(ev_conv_w[0].astype(F32), ((0, SUBLANE - D_CONV), (0, 0))),
        "conv_b": _row(ev_conv_b[0]),
        "dt_bias": _row(jnp.concatenate([ev_dt_bias_f[0], ev_dt_bias_b[0]]), LANE),
        "a_row": _row(-jnp.exp(jnp.concatenate([ev_a_log_f[0], ev_a_log_b[0]]).astype(F32)), LANE),
        "d_skip": _row(jnp.repeat(ev_d_skip[0].astype(F32), SSM_HEAD_DIM)),
        "ssm_norm": _row(ev_ssm_norm[0]),
        "w_out": ev_w_out[0].astype(BF16),
        "w_mix": od_w_mix[0].astype(BF16),
    }
    mem = jnp.concatenate([mem_prompt, mem_sample], axis=0)
    kv_layers = [_memkv(mem, _row(norm_mem[i]), xa_wkv[i].astype(BF16)) for i in range(depth)]
    y_prompt = _trunk(x_prompt, kv_layers, 0, p)
    y_sample = _trunk(x_sample, kv_layers, x_prompt.shape[0], p)
    return (y_prompt, y_sample)
```

```python
import functools
import math

import jax
import jax.numpy as jnp
from jax import lax
from jax.experimental import pallas as pl
from jax.experimental.pallas import tpu as pltpu

F32 = jnp.float32
BF16 = jnp.bfloat16

EPS = 1e-6
D_MODEL = 1024
N_MEM = 256

SSM_HEADS = 8
SSM_HEAD_DIM = 64
SSM_D = SSM_HEADS * SSM_HEAD_DIM
SSM_GROUPS = 2
SSM_HPG = SSM_HEADS // SSM_GROUPS
SSM_STATE = 128
D_CONV = 5
CONV_CH = SSM_D + 2 * SSM_GROUPS * SSM_STATE
CHUNK = 128

MLA_HEADS = 8
QK_NOPE = 64
QK_ROPE = 32
V_DIM = 64
Q_LORA = 256
KV_LORA = 128
ROPE_THETA = 10000.0
MLA_D = MLA_HEADS * V_DIM

FOURIER_GROUPS = 4
FOURIER_GROUP_DIM = D_MODEL // FOURIER_GROUPS

XA_HEADS = 4
XA_HEAD_DIM = D_MODEL // XA_HEADS

LANE = 128
SUBLANE = 8
HEAD_PAD = LANE
V_ONE_LANE = V_DIM
VMEM_LIMIT = 56 * 1024 * 1024

TOKEN_TILE = 512
FFN_CHUNK = 256
ATTN_TQ = 1024
ATTN_TK = 2048
ATTN_UNROLL = 2
FNET_COLS = 8
SSD_CHUNKS_PER_STEP = 4
LOG2E = 1.4426950408889634

O_Z = 0
O_XBC = O_Z + SSM_D
O_CQ = O_XBC + CONV_CH
O_CKV = O_CQ + Q_LORA
O_KA = O_CKV + KV_LORA
O_KB = O_KA + LANE
O_DT = O_KB + LANE
D_IN_PAD = O_DT + LANE


def _params(sem, vmem=VMEM_LIMIT):
    return pltpu.CompilerParams(dimension_semantics=sem, vmem_limit_bytes=vmem)


def _rms(x, w):
    ms = jnp.mean(x * x, axis=-1, keepdims=True)
    return x * lax.rsqrt(ms + EPS) * w


def _silu(x):
    return x / (1.0 + jnp.exp(-x))


def _dot(a, b):
    return jnp.dot(a, b, preferred_element_type=F32)


def _dot_nt(a, b):
    return lax.dot_general(a, b, (((1,), (1,)), ((), ())), preferred_element_type=F32)


def _const_spec(shape):
    nd = len(shape)
    return pl.BlockSpec(shape, lambda *_: (0,) * nd)


def _memkv_kernel(mem_ref, nw_ref, wkv_ref, kv_ref):
    xn = _rms(mem_ref[0], nw_ref[...]).astype(BF16)
    kv_ref[0] = _dot(xn, wkv_ref[...]).astype(BF16)


def _memkv(mem, nw, wkv):
    b, m, d = mem.shape
    return pl.pallas_call(
        _memkv_kernel,
        grid=(b,),
        in_specs=[pl.BlockSpec((1, m, d), lambda i: (i, 0, 0)),
                  _const_spec((1, d)),
                  _const_spec((d, 2 * d))],
        out_specs=pl.BlockSpec((1, m, 2 * d), lambda i: (i, 0, 0)),
        out_shape=jax.ShapeDtypeStruct((b, m, 2 * d), BF16),
        compiler_params=_params(("parallel",)),
        name="memkv",
    )(mem, nw, wkv)


def _inproj_kernel(xp_ref, x_ref, xn_ref, nw_ref, win_ref, qnw_ref, kvnw_ref, wqm_ref, wqr_ref,
                   wk_ref, wv_ref, cos_ref, sin_ref, cw_ref, cb_ref,
                   z_ref, xbc_ref, dt_ref, q_ref, k_ref, v_ref, *, tiles_per_seq):
    t = x_ref.shape[0]
    pos = pl.program_id(0) % tiles_per_seq
    x_ext = jnp.concatenate([xp_ref[...], x_ref[...], xn_ref[...]], axis=0)
    hn = _rms(x_ext, nw_ref[...]).astype(BF16)
    proj_ext = _dot(hn, win_ref[...])
    proj = proj_ext[SUBLANE:SUBLANE + t, :]
    z_ref[...] = proj[:, O_Z:O_XBC]
    dt_ref[...] = proj[:, O_DT:D_IN_PAD]
    row = lax.broadcasted_iota(jnp.int32, (t + 2 * SUBLANE, 1), 0)
    inside = ((row >= SUBLANE) | (pos > 0)) & ((row < SUBLANE + t) | (pos < tiles_per_seq - 1))
    xbc_ext = jnp.where(inside, proj_ext[:, O_XBC:O_CQ], 0.0)
    acc = cb_ref[...] + jnp.zeros((t, CONV_CH), F32)
    n_ext = t + 2 * SUBLANE
    for j in range(D_CONV):
        tap = pltpu.roll(xbc_ext, (D_CONV // 2 - j) % n_ext, 0)[SUBLANE:SUBLANE + t, :]
        acc = acc + tap * cw_ref[j:j + 1, :]
    xbc_ref[...] = _silu(acc).astype(BF16)
    cqn = _rms(proj[:, O_CQ:O_CKV], qnw_ref[...]).astype(BF16)
    ckvn = _rms(proj[:, O_CKV:O_KA], kvnw_ref[...]).astype(BF16)
    cos_t = cos_ref[...]
    sin_t = sin_ref[...]
    qscale = LOG2E * (QK_NOPE + QK_ROPE) ** -0.5
    cos_q = cos_t * qscale
    sin_q = sin_t * qscale
    qm = _dot(cqn, wqm_ref[...])
    qr = _dot(cqn, wqr_ref[...])
    kr = proj[:, O_KA:O_KB] * cos_t + proj[:, O_KB:O_DT] * sin_t
    km = _dot(ckvn, wk_ref[...])
    vm = _dot(ckvn, wv_ref[...])
    lane = lax.broadcasted_iota(jnp.int32, (1, HEAD_PAD), 1)
    one_col = jnp.where(lane == V_ONE_LANE, 1.0, 0.0).astype(F32)
    for h in range(MLA_HEADS):
        sl = slice(h * HEAD_PAD, (h + 1) * HEAD_PAD)
        q_ref[:, sl] = (qm[:, sl] * cos_q + qr[:, sl] * sin_q).astype(BF16)
        k_ref[:, sl] = (km[:, sl] + kr).astype(BF16)
        v_ref[:, sl] = (vm[:, sl] + one_col).astype(BF16)


def _inproj(x2, nw, win, qnw, kvnw, wqm, wqr, wk, wv, cos_t, sin_t, cw, cb, seq):
    n, d = x2.shape
    t = TOKEN_TILE
    tiles_per_seq = seq // t
    rb = t // SUBLANE
    nrb = n // SUBLANE
    tok = lambda w: pl.BlockSpec((t, w), lambda i: (i, 0))
    halo_prev = pl.BlockSpec((SUBLANE, d), lambda i: (jnp.maximum(i * rb - 1, 0), 0))
    halo_next = pl.BlockSpec((SUBLANE, d), lambda i: (jnp.minimum(i * rb + rb, nrb - 1), 0))
    pos = pl.BlockSpec((t, HEAD_PAD), lambda i: (i % tiles_per_seq, 0))
    hp = MLA_HEADS * HEAD_PAD
    outs = [jax.ShapeDtypeStruct((n, SSM_D), F32), jax.ShapeDtypeStruct((n, CONV_CH), BF16),
            jax.ShapeDtypeStruct((n, LANE), F32), jax.ShapeDtypeStruct((n, hp), BF16),
            jax.ShapeDtypeStruct((n, hp), BF16), jax.ShapeDtypeStruct((n, hp), BF16)]
    return pl.pallas_call(
        functools.partial(_inproj_kernel, tiles_per_seq=tiles_per_seq),
        grid=(n // t,),
        in_specs=[halo_prev, tok(d), halo_next, _const_spec((1, d)), _const_spec((d, D_IN_PAD)),
                  _const_spec((1, Q_LORA)), _const_spec((1, KV_LORA)),
                  _const_spec((Q_LORA, hp)), _const_spec((Q_LORA, hp)),
                  _const_spec((KV_LORA, hp)), _const_spec((KV_LORA, hp)),
                  pos, pos, _const_spec((SUBLANE, CONV_CH)), _const_spec((1, CONV_CH))],
        out_specs=[tok(SSM_D), tok(CONV_CH), tok(LANE), tok(hp), tok(hp), tok(hp)],
        out_shape=outs,
        compiler_params=_params(("parallel",)),
        name="inproj",
    )(x2, x2, x2, nw, win, qnw, kvnw, wqm, wqr, wk, wv, cos_t, sin_t, cw, cb)


def _softplus(x):
    return jnp.maximum(x, 0.0) + jnp.log1p(jnp.exp(-jnp.abs(x)))


def _dot_pieces(a, passes, fn):
    out = None
    rem = a
    for _ in range(passes):
        piece = rem.astype(BF16)
        rem = rem - piece.astype(F32)
        term = fn(piece)
        out = term if out is None else out + term
    return out


def _ssd_chunk(xbc, dt_raw, dtb_ref, a_ref, e64_ref, h_in, reverse, lane_off):
    L = CHUNK
    P = SSM_HEAD_DIM
    gw = SSM_HPG * P
    xs = xbc[:, :SSM_D].astype(F32)
    bm = xbc[:, SSM_D:SSM_D + SSM_GROUPS * SSM_STATE]
    cm = xbc[:, SSM_D + SSM_GROUPS * SSM_STATE:]
    dt = _softplus(dt_raw + dtb_ref[...])
    dta = dt * a_ref[...]
    row = lax.broadcasted_iota(jnp.int32, (L, L), 0)
    col = lax.broadcasted_iota(jnp.int32, (L, L), 1)
    mask = (col >= row) if reverse else (col <= row)
    tri = jnp.where(mask, 1.0, 0.0).astype(BF16)
    cum = _dot_pieces(dta, 3, lambda piece: _dot(tri, piece))
    cum_t = cum.T
    dt_t = dt.T
    last = 0 if reverse else L - 1
    exp_cum = jnp.exp(cum)
    w_state = dt * jnp.exp(cum[last:last + 1, :] - cum)
    scal64 = _dot_pieces(jnp.concatenate([w_state, exp_cum], axis=0), 2,
                         lambda piece: _dot(piece, e64_ref[...]))
    xw = (xs * scal64[:L]).astype(BF16)
    exp_cum64 = scal64[L:]
    exp_total64 = exp_cum64[last:last + 1, :]
    lane = lax.broadcasted_iota(jnp.int32, (L, 2 * P), 1)
    ys = []
    h_out = []
    for g in range(SSM_GROUPS):
        bg = bm[:, g * SSM_STATE:(g + 1) * SSM_STATE]
        cg = cm[:, g * SSM_STATE:(g + 1) * SSM_STATE]
        cb = _dot_nt(cg, bg)
        h_prev = h_in[g]
        y_off = _dot(cg, h_prev.astype(BF16)) * exp_cum64[:, g * gw:(g + 1) * gw]
        pairs = []
        for pr in range(SSM_HPG // 2):
            x_pair = xbc[:, g * gw + pr * 2 * P:g * gw + (pr + 1) * 2 * P]
            halves = []
            for q in range(2):
                hh = g * SSM_HPG + pr * 2 + q
                ln = lane_off + hh
                seg = cum[:, ln:ln + 1] - cum_t[ln:ln + 1, :]
                dec = jnp.exp(jnp.where(mask, seg, -jnp.inf))
                m = (cb * dec * dt_t[ln:ln + 1, :]).astype(BF16)
                halves.append(_dot(m, x_pair))
            pairs.append(jnp.where(lane < P, halves[0], halves[1]))
        ys.append(y_off + jnp.concatenate(pairs, axis=1))
        st = _dot(bg.astype(F32).T.astype(BF16), xw[:, g * gw:(g + 1) * gw])
        h_out.append(h_prev * exp_total64[:, g * gw:(g + 1) * gw] + st)
    return jnp.concatenate(ys, axis=1), xs, h_out


def _ssd_kernel(xf_ref, xb_ref, dtf_ref, dtb_ref, bias_ref, a_ref, dskip_ref,
                e64f_ref, e64b_ref, yf_ref, yb_ref, hf_ref, hb_ref):
    @pl.when(pl.program_id(1) == 0)
    def _():
        hf_ref[...] = jnp.zeros_like(hf_ref)
        hb_ref[...] = jnp.zeros_like(hb_ref)

    n_sub = xf_ref.shape[1] // CHUNK
    h_f = [hf_ref[g] for g in range(SSM_GROUPS)]
    h_b = [hb_ref[g] for g in range(SSM_GROUPS)]
    for i in range(n_sub):
        rows = slice(i * CHUNK, (i + 1) * CHUNK)
        y_f, xs_f, h_f = _ssd_chunk(xf_ref[0, rows, :], dtf_ref[0, rows, :], bias_ref, a_ref,
                                    e64f_ref, h_f, False, 0)
        yf_ref[0, rows, :] = y_f + xs_f * dskip_ref[...]
        rows = slice((n_sub - 1 - i) * CHUNK, (n_sub - i) * CHUNK)
        y_b, _, h_b = _ssd_chunk(xb_ref[0, rows, :], dtb_ref[0, rows, :], bias_ref, a_ref,
                                 e64b_ref, h_b, True, SSM_HEADS)
        yb_ref[0, rows, :] = y_b
    for g in range(SSM_GROUPS):
        hf_ref[g] = h_f[g]
        hb_ref[g] = h_b[g]


def _head_lane_expander(lane_off, width):
    src_lane = lax.broadcasted_iota(jnp.int32, (LANE, SSM_HEADS * width), 0)
    dst_head = lax.broadcasted_iota(jnp.int32, (LANE, SSM_HEADS * width), 1) // width
    return (src_lane == dst_head + lane_off).astype(BF16)


def _ssd(xbc, dt, bias, a_row, dskip):
    b, s, _ = xbc.shape
    expanders = [_head_lane_expander(off, SSM_HEAD_DIM) for off in (0, SSM_HEADS)]
    rows = SSD_CHUNKS_PER_STEP * CHUNK
    nc = s // rows
    chunk = lambda w, f: pl.BlockSpec((1, rows, w), lambda i, c: (i, f(c), 0))
    fwd = lambda c: c
    bwd = lambda c: nc - 1 - c
    hshape = (SSM_GROUPS, SSM_STATE, SSM_HPG * SSM_HEAD_DIM)
    return pl.pallas_call(
        _ssd_kernel,
        grid=(b, nc),
        in_specs=[chunk(CONV_CH, fwd), chunk(CONV_CH, bwd), chunk(LANE, fwd), chunk(LANE, bwd),
                  _const_spec((1, LANE)), _const_spec((1, LANE)), _const_spec((1, SSM_D))]
        + [_const_spec(e.shape) for e in expanders],
        out_specs=[chunk(SSM_D, fwd), chunk(SSM_D, bwd)],
        out_shape=[jax.ShapeDtypeStruct((b, s, SSM_D), F32)] * 2,
        scratch_shapes=[pltpu.VMEM(hshape, F32), pltpu.VMEM(hshape, F32)],
        compiler_params=_params(("parallel", "arbitrary")),
        name="ssd",
    )(xbc, xbc, dt, dt, bias, a_row, dskip, *expanders)


def _flash_kernel(q_ref, k_ref, v_ref, o_ref, *, tk):
    s_len = k_ref.shape[1]
    tq = q_ref.shape[1]
    sls = [slice(hh * HEAD_PAD, (hh + 1) * HEAD_PAD) for hh in range(2)]

    def body(j, carry):
        off = pl.multiple_of(j * tk, tk)
        new = []
        for sl, (m, acc) in zip(sls, carry):
            kj = k_ref[0, pl.ds(off, tk), sl]
            vj = v_ref[0, pl.ds(off, tk), sl]
            s = _dot_nt(q_ref[0, :, sl], kj)
            m_new = jnp.maximum(m, jnp.max(s, axis=1, keepdims=True))
            alpha = jnp.exp2(m - m_new)
            p = jnp.exp2(s - m_new).astype(BF16)
            new.append((m_new, acc * alpha + _dot(p, vj)))
        return tuple(new)

    m0 = jnp.full((tq, 1), -jnp.inf, F32)
    acc0 = jnp.zeros((tq, HEAD_PAD), F32)
    carry = lax.fori_loop(0, s_len // tk, body, ((m0, acc0), (m0, acc0)), unroll=ATTN_UNROLL)
    outs = [acc / acc[:, V_ONE_LANE:V_ONE_LANE + 1] for _, acc in carry]
    lane = lax.broadcasted_iota(jnp.int32, (tq, HEAD_PAD), 1)
    o_ref[0] = jnp.where(lane < V_DIM, outs[0], pltpu.roll(outs[1], V_DIM, 1)).astype(BF16)


def _flash(q, k, v):
    b, s, hp = q.shape
    tq = min(ATTN_TQ, s)
    tk = min(ATTN_TK, s)
    pairs = MLA_HEADS // 2
    pw = 2 * HEAD_PAD
    return pl.pallas_call(
        functools.partial(_flash_kernel, tk=tk),
        grid=(b, pairs, s // tq),
        in_specs=[pl.BlockSpec((1, tq, pw), lambda i, p, j: (i, j, p)),
                  pl.BlockSpec((1, s, pw), lambda i, p, j: (i, 0, p)),
                  pl.BlockSpec((1, s, pw), lambda i, p, j: (i, 0, p))],
        out_specs=pl.BlockSpec((1, tq, 2 * V_DIM), lambda i, p, j: (i, j, p)),
        out_shape=jax.ShapeDtypeStruct((b, s, MLA_D), BF16),
        compiler_params=_params(("parallel", "parallel", "arbitrary")),
        name="mla_flash",
    )(q, k, v)


def _cross_attn(h1, kv_ref, pre_w, wq_ref, wo_ref, post_w):
    hn = _rms(h1, pre_w).astype(BF16)
    q = (_dot(hn, wq_ref[...]) * (XA_HEAD_DIM ** -0.5)).astype(BF16)
    heads = []
    for hd in range(XA_HEADS):
        sl = slice(hd * XA_HEAD_DIM, (hd + 1) * XA_HEAD_DIM)
        kh = kv_ref[0, :, sl]
        vh = kv_ref[0, :, D_MODEL + hd * XA_HEAD_DIM:D_MODEL + (hd + 1) * XA_HEAD_DIM]
        s = _dot_nt(q[:, sl], kh)
        p = jnp.exp(s - jnp.max(s, axis=1, keepdims=True))
        l = jnp.sum(p, axis=1, keepdims=True)
        heads.append((_dot(p.astype(BF16), vh) / l).astype(BF16))
    o = jnp.concatenate(heads, axis=1)
    xa = _dot(o, wo_ref[...])
    return h1 + _rms(xa, post_w)


def _postmix_even_kernel(h_ref, yf_ref, yb_ref, z_ref, o_ref, snw_ref, wout_ref, mpost_ref,
                         kv_ref, xpre_ref, wq_ref, wo_ref, xpost_ref, out_ref):
    y = (yf_ref[0] + yb_ref[0]) * _silu(z_ref[0])
    gw = SSM_D // SSM_GROUPS
    parts = []
    for g in range(SSM_GROUPS):
        yg = y[:, g * gw:(g + 1) * gw]
        parts.append(yg * lax.rsqrt(jnp.mean(yg * yg, axis=-1, keepdims=True) + EPS))
    y_ssd = (jnp.concatenate(parts, axis=1) * snw_ref[...]).astype(BF16)
    mix = _dot(y_ssd, wout_ref[:SSM_D, :]) + _dot(o_ref[0], wout_ref[SSM_D:, :])
    h1 = h_ref[0] + _rms(mix, mpost_ref[...])
    out_ref[0] = _cross_attn(h1, kv_ref, xpre_ref[...], wq_ref, wo_ref, xpost_ref[...])


def _postmix_odd_kernel(h_ref, f_ref, wmix_ref, mpost_ref,
                        kv_ref, xpre_ref, wq_ref, wo_ref, xpost_ref, out_ref):
    mix = _dot(f_ref[0], wmix_ref[...])
    h1 = h_ref[0] + _rms(mix, mpost_ref[...])
    out_ref[0] = _cross_attn(h1, kv_ref, xpre_ref[...], wq_ref, wo_ref, xpost_ref[...])


def _postmix(kernel, h, mixed, mixed_w, consts_a, kv, kv_off, consts_b):
    b, s, d = h.shape
    t = TOKEN_TILE
    tok = lambda w: pl.BlockSpec((1, t, w), lambda i, j: (i, j, 0))
    in_specs = [tok(d)] + [tok(w) for w in mixed_w]
    in_specs += [_const_spec(c.shape) for c in consts_a]
    in_specs += [pl.BlockSpec((1, N_MEM, 2 * d), lambda i, j: (i + kv_off, 0, 0))]
    in_specs += [_const_spec(c.shape) for c in consts_b]
    return pl.pallas_call(
        kernel,
        grid=(b, s // t),
        in_specs=in_specs,
        out_specs=tok(d),
        out_shape=jax.ShapeDtypeStruct((b, s, d), F32),
        compiler_params=_params(("parallel", "parallel")),
        name="postmix",
    )(h, *mixed, *consts_a, kv, *consts_b)


def _ffn_kernel(h_ref, pre_ref, wg_ref, wu_ref, wd_ref, post_ref, *rest):
    out_ref = rest[-2] if len(rest) == 3 else rest[0]
    h = h_ref[...]
    hn = _rms(h, pre_ref[...]).astype(BF16)
    d_ff = wg_ref.shape[1]
    acc = jnp.zeros(h.shape, F32)
    for c in range(d_ff // FFN_CHUNK):
        sl = slice(c * FFN_CHUNK, (c + 1) * FFN_CHUNK)
        g = _dot(hn, wg_ref[:, sl])
        u = _dot(hn, wu_ref[:, sl])
        acc = acc + _dot((_silu(g) * u).astype(BF16), wd_ref[sl, :])
    out = h + _rms(acc, post_ref[...])
    out_ref[...] = out
    if len(rest) == 3:
        rest[2][...] = _rms(out, rest[0][...]).astype(BF16)


def _ffn(h2, pre, wg, wu, wd, post, next_pre=None):
    n, d = h2.shape
    t = TOKEN_TILE
    tok = pl.BlockSpec((t, d), lambda i: (i, 0))
    in_specs = [tok, _const_spec((1, d)), _const_spec(wg.shape), _const_spec(wu.shape),
                _const_spec(wd.shape), _const_spec((1, d))]
    args = [h2, pre, wg, wu, wd, post]
    out_specs, out_shape = tok, jax.ShapeDtypeStruct((n, d), F32)
    if next_pre is not None:
        in_specs.append(_const_spec((1, d)))
        args.append(next_pre)
        out_specs, out_shape = [tok, tok], [out_shape, jax.ShapeDtypeStruct((n, d), BF16)]
    return pl.pallas_call(
        _ffn_kernel,
        grid=(n // t,),
        in_specs=in_specs,
        out_specs=out_specs,
        out_shape=out_shape,
        compiler_params=_params(("parallel",)),
        name="ffn",
    )(*args)


def _fnet_a_kernel(x_ref, cs_ref, m1_ref, a_ref):
    gd = FOURIER_GROUP_DIM
    for c in range(x_ref.shape[2] // D_MODEL):
        cl = slice(c * D_MODEL, (c + 1) * D_MODEL)
        xn = x_ref[0, :, cl]
        yr, yi = [], []
        for g in range(FOURIER_GROUPS):
            y = _dot(xn[:, g * gd:(g + 1) * gd], cs_ref[...])
            yr.append(y[:, :gd])
            yi.append(y[:, gd:])
        stack = jnp.concatenate([jnp.concatenate(yr, axis=1), jnp.concatenate(yi, axis=1)], axis=0)
        a_ref[0, :, cl] = _dot(m1_ref[...], stack.astype(BF16)).astype(BF16)


def _fnet_b_kernel(a_ref, g_ref, f_ref):
    for c in range(g_ref.shape[0]):
        stack = jnp.concatenate([a_ref[0, 0, c], a_ref[0, 1, c]], axis=0)
        f_ref[0, :, c * D_MODEL:(c + 1) * D_MODEL] = _dot(g_ref[c], stack).astype(BF16)


def _fnet(xn, cs, m1, gt):
    b, s, d = xn.shape
    n1 = m1.shape[0] // 2
    n2 = s // n1
    ca = min(FNET_COLS, n2)
    cb = min(FNET_COLS, n1)
    a = pl.pallas_call(
        _fnet_a_kernel,
        grid=(b, n2 // ca),
        in_specs=[pl.BlockSpec((1, n1, ca * d), lambda i, j: (i, 0, j)),
                  _const_spec(cs.shape), _const_spec(m1.shape)],
        out_specs=pl.BlockSpec((1, 2 * n1, ca * d), lambda i, j: (i, 0, j)),
        out_shape=jax.ShapeDtypeStruct((b, 2 * n1, n2 * d), BF16),
        compiler_params=_params(("parallel", "parallel")),
        name="fnet_a",
    )(xn.reshape(b, n1, n2 * d), cs, m1)
    f = pl.pallas_call(
        _fnet_b_kernel,
        grid=(b, n1 // cb),
        in_specs=[pl.BlockSpec((1, 2, cb, n2, d), lambda i, j: (i, 0, j, 0, 0)),
                  pl.BlockSpec((cb, n2, 2 * n2), lambda i, j: (j, 0, 0))],
        out_specs=pl.BlockSpec((1, n2, cb * d), lambda i, j: (i, 0, j)),
        out_shape=jax.ShapeDtypeStruct((b, n2, n1 * d), BF16),
        compiler_params=_params(("parallel", "parallel")),
        name="fnet_b",
    )(a.reshape(b, 2, n1, n2, d), gt)
    return f.reshape(b, s, d)


def _rope_tables(s):
    inv = ROPE_THETA ** (-jnp.arange(0, QK_ROPE, 2, dtype=F32) / QK_ROPE)
    ang = jnp.arange(s, dtype=F32)[:, None] * inv[None, :]
    cos2 = jnp.concatenate([jnp.cos(ang), jnp.cos(ang)], axis=1)
    sin2 = jnp.concatenate([jnp.sin(ang), jnp.sin(ang)], axis=1)
    pad = HEAD_PAD - QK_NOPE - QK_ROPE
    cos_t = jnp.concatenate([jnp.ones((s, QK_NOPE), F32), cos2, jnp.ones((s, pad), F32)], axis=1)
    sin_t = jnp.concatenate([jnp.zeros((s, QK_NOPE), F32), sin2, jnp.zeros((s, pad), F32)], axis=1)
    return cos_t, sin_t


def _rot_cols(w):
    half = w.shape[-1] // 2
    return jnp.concatenate([-w[..., half:], w[..., :half]], axis=-1)


def _pad_cols(w, left, total):
    return jnp.pad(w, ((0, 0), (left, total - left - w.shape[1])))


def _even_weights(w_in, w_uq, w_ukv):
    o1 = SSM_D
    o2 = o1 + CONV_CH
    o3 = o2 + 2 * SSM_HEADS
    o4 = o3 + Q_LORA
    o5 = o4 + KV_LORA
    w_z, w_xbc, w_dt, w_cq, w_ckv, w_kr = (w_in[:, :o1], w_in[:, o1:o2], w_in[:, o2:o3],
                                             w_in[:, o3:o4], w_in[:, o4:o5], w_in[:, o5:])
    win = jnp.concatenate([
        w_z, w_xbc, w_cq, w_ckv,
        _pad_cols(w_kr, QK_NOPE, LANE), _pad_cols(_rot_cols(w_kr), QK_NOPE, LANE),
        _pad_cols(w_dt, 0, LANE)], axis=1).astype(BF16)
    dq = QK_NOPE + QK_ROPE
    wq = w_uq.reshape(Q_LORA, MLA_HEADS, dq)
    zq = jnp.zeros((Q_LORA, MLA_HEADS, HEAD_PAD - dq), F32)
    wqm = jnp.concatenate([wq, zq], axis=-1).reshape(Q_LORA, -1).astype(BF16)
    wqr = jnp.concatenate([jnp.zeros((Q_LORA, MLA_HEADS, QK_NOPE), F32),
                           _rot_cols(wq[..., QK_NOPE:]), zq], axis=-1).reshape(Q_LORA, -1).astype(BF16)
    wkv = w_ukv.reshape(KV_LORA, MLA_HEADS, QK_NOPE + V_DIM)
    zk = jnp.zeros((KV_LORA, MLA_HEADS, HEAD_PAD - QK_NOPE), F32)
    wk = jnp.concatenate([wkv[..., :QK_NOPE], zk], axis=-1).reshape(KV_LORA, -1).astype(BF16)
    zv = jnp.zeros((KV_LORA, MLA_HEADS, HEAD_PAD - V_DIM), F32)
    wv = jnp.concatenate([wkv[..., QK_NOPE:], zv], axis=-1).reshape(KV_LORA, -1).astype(BF16)
    return win, wqm, wqr, wk, wv


def _fnet_tables(s):
    n2 = CHUNK
    n1 = s // n2
    gd = FOURIER_GROUP_DIM
    ci = jnp.arange(gd, dtype=jnp.int32)
    ang_c = (2.0 * math.pi / gd) * ((ci[:, None] * ci[None, :]) % gd).astype(F32)
    cs = (jnp.concatenate([jnp.cos(ang_c), -jnp.sin(ang_c)], axis=1) * gd ** -0.5).astype(BF16)
    i1 = jnp.arange(n1, dtype=jnp.int32)
    ang1 = (2.0 * math.pi / n1) * ((i1[:, None] * i1[None, :]) % n1).astype(F32)
    c1, s1 = jnp.cos(ang1), jnp.sin(ang1)
    m1 = jnp.concatenate([jnp.concatenate([c1, s1], axis=1),
                          jnp.concatenate([-s1, c1], axis=1)], axis=0).astype(BF16)
    k1 = jnp.arange(n1, dtype=jnp.int32)[:, None, None]
    k2 = jnp.arange(n2, dtype=jnp.int32)[None, :, None]
    j2 = jnp.arange(n2, dtype=jnp.int32)[None, None, :]
    ang = (2.0 * math.pi / s) * ((j2 * (k1 + n1 * k2)) % s).astype(F32)
    gt = (jnp.concatenate([jnp.cos(ang), jnp.sin(ang)], axis=2) * s ** -0.5).astype(BF16)
    return cs, m1, gt


def _row(v, width=None):
    v = v.astype(F32).reshape(1, -1)
    if width is not None:
        v = jnp.pad(v, ((0, 0), (0, width - v.shape[1])))
    return v


def _trunk(x, kv_layers, kv_off, p):
    b, s, d = x.shape
    n = b * s
    h = x
    assert d == D_MODEL and s % min(ATTN_TQ, s) == 0 and s % min(ATTN_TK, s) == 0
    assert s % TOKEN_TILE == 0 and s % (SSD_CHUNKS_PER_STEP * CHUNK) == 0 and (s // CHUNK) % SUBLANE == 0
    z, xbc, dt, q, k, v = _inproj(h.reshape(n, d), p["mix_pre"][0], p["win"], p["q_norm"], p["kv_norm"],
                                  p["wqm"], p["wqr"], p["wk"], p["wv"], p["cos"][:s], p["sin"][:s],
                                  p["conv_w"], p["conv_b"], s)
    yf, yb = _ssd(xbc.reshape(b, s, -1), dt.reshape(b, s, -1), p["dt_bias"], p["a_row"], p["d_skip"])
    hp = MLA_HEADS * HEAD_PAD
    o = _flash(q.reshape(b, s, hp), k.reshape(b, s, hp), v.reshape(b, s, hp))
    h = _postmix(_postmix_even_kernel, h, (yf, yb, z.reshape(b, s, -1), o),
                 (SSM_D, SSM_D, SSM_D, MLA_D),
                 (p["ssm_norm"], p["w_out"], p["mix_post"][0]), kv_layers[0], kv_off,
                 (p["xa_pre"][0], p["xa_wq"][0], p["xa_wo"][0], p["xa_post"][0]))
    h, hn = _ffn(h.reshape(n, d), p["ffn_pre"][0], p["wg"][0], p["wu"][0], p["wd"][0],
                 p["ffn_post"][0], next_pre=p["mix_pre"][1])
    h = h.reshape(b, s, d)
    cs, m1, gt = _fnet_tables(s)
    f = _fnet(hn.reshape(b, s, d), cs, m1, gt)
    h = _postmix(_postmix_odd_kernel, h, (f,), (d,), (p["w_mix"], p["mix_post"][1]),
                 kv_layers[1], kv_off,
                 (p["xa_pre"][1], p["xa_wq"][1], p["xa_wo"][1], p["xa_post"][1]))
    h = _ffn(h.reshape(n, d), p["ffn_pre"][1], p["wg"][1], p["wu"][1], p["wd"][1],
             p["ffn_post"][1]).reshape(b, s, d)
    return h


def kernel(x_prompt, x_sample, mem_prompt, mem_sample, norm_mix_pre, norm_mix_post, norm_xa_pre, norm_xa_post, norm_mem, xa_wq, xa_wkv, xa_wo, norm_ffn_pre, norm_ffn_post, ffn_w_gu, ffn_w_down, ev_w_in, ev_conv_w, ev_conv_b, ev_a_log_f, ev_a_log_b, ev_dt_bias_f, ev_dt_bias_b, ev_d_skip, ev_ssm_norm, ev_q_norm, ev_w_uq, ev_kv_norm, ev_w_ukv, ev_w_out, od_w_mix):
    depth = norm_mix_pre.shape[0]
    assert depth == 2 and ev_w_in.shape[0] == 1 and od_w_mix.shape[0] == 1
    d_ff = ffn_w_down.shape[1]
    assert d_ff % FFN_CHUNK == 0
    s_max = max(x_prompt.shape[1], x_sample.shape[1])
    cos_t, sin_t = _rope_tables(s_max)
    win, wqm, wqr, wk, wv = _even_weights(ev_w_in[0], ev_w_uq[0], ev_w_ukv[0])
    rows = lambda w: [_row(w[i]) for i in range(depth)]
    p = {
        "mix_pre": rows(norm_mix_pre), "mix_post": rows(norm_mix_post),
        "xa_pre": rows(norm_xa_pre), "xa_post": rows(norm_xa_post),
        "ffn_pre": rows(norm_ffn_pre), "ffn_post": rows(norm_ffn_post),
        "xa_wq": [xa_wq[i].astype(BF16) for i in range(depth)],
        "xa_wo": [xa_wo[i].astype(BF16) for i in range(depth)],
        "wg": [ffn_w_gu[i, :, :d_ff].astype(BF16) for i in range(depth)],
        "wu": [ffn_w_gu[i, :, d_ff:].astype(BF16) for i in range(depth)],
        "wd": [ffn_w_down[i].astype(BF16) for i in range(depth)],
        "win": win, "wqm": wqm, "wqr": wqr, "wk": wk, "wv": wv,
        "q_norm": _row(ev_q_norm[0]), "kv_norm": _row(ev_kv_norm[0]),
        "cos": cos_t, "sin": sin_t,
        "conv_w": jnp.pad(ev_conv_w[0].astype(F32), ((0, SUBLANE - D_CONV), (0, 0))),
        "conv_b": _row(ev_conv_b[0]),
        "dt_bias": _row(jnp.concatenate([ev_dt_bias_f[0], ev_dt_bias_b[0]]), LANE),
        "a_row": _row(-jnp.exp(jnp.concatenate([ev_a_log_f[0], ev_a_log_b[0]]).astype(F32)), LANE),
        "d_skip": _row(jnp.repeat(ev_d_skip[0].astype(F32), SSM_HEAD_DIM)),
        "ssm_norm": _row(ev_ssm_norm[0]),
        "w_out": ev_w_out[0].astype(BF16),
        "w_mix": od_w_mix[0].astype(BF16),
    }
    mem = jnp.concatenate([mem_prompt, mem_sample], axis=0)
    kv_layers = [_memkv(mem, _row(norm_mem[i]), xa_wkv[i].astype(BF16)) for i in range(depth)]
    y_prompt = _trunk(x_prompt, kv_layers, 0, p)
    y_sample = _trunk(x_sample, kv_layers, x_prompt.shape[0], p)
    return (y_prompt, y_sample)
```

```python
import functools
import math

import jax
import jax.numpy as jnp
from jax import lax
from jax.experimental import pallas as pl
from jax.experimental.pallas import tpu as pltpu

F32 = jnp.float32
BF16 = jnp.bfloat16

EPS = 1e-6
D_MODEL = 1024
N_MEM = 256

SSM_HEADS = 8
SSM_HEAD_DIM = 64
SSM_D = SSM_HEADS * SSM_HEAD_DIM
SSM_GROUPS = 2
SSM_HPG = SSM_HEADS // SSM_GROUPS
SSM_STATE = 128
D_CONV = 5
CONV_CH = SSM_D + 2 * SSM_GROUPS * SSM_STATE
CHUNK = 128

MLA_HEADS = 8
QK_NOPE = 64
QK_ROPE = 32
V_DIM = 64
Q_LORA = 256
KV_LORA = 128
ROPE_THETA = 10000.0
MLA_D = MLA_HEADS * V_DIM

FOURIER_GROUPS = 4
FOURIER_GROUP_DIM = D_MODEL // FOURIER_GROUPS

XA_HEADS = 4
XA_HEAD_DIM = D_MODEL // XA_HEADS

LANE = 128
SUBLANE = 8
HEAD_PAD = LANE
V_ONE_LANE = V_DIM
VMEM_LIMIT = 56 * 1024 * 1024

TOKEN_TILE = 512
FFN_CHUNK = 256
ATTN_TQ = 1024
ATTN_TK = 2048
ATTN_UNROLL = 2
FNET_COLS = 8
SSD_CHUNKS_PER_STEP = 4
LOG2E = 1.4426950408889634

O_Z = 0
O_XBC = O_Z + SSM_D
O_CQ = O_XBC + CONV_CH
O_CKV = O_CQ + Q_LORA
O_KA = O_CKV + KV_LORA
O_KB = O_KA + LANE
O_DT = O_KB + LANE
D_IN_PAD = O_DT + LANE


def _params(sem, vmem=VMEM_LIMIT):
    return pltpu.CompilerParams(dimension_semantics=sem, vmem_limit_bytes=vmem)


def _rms(x, w):
    ms = jnp.mean(x * x, axis=-1, keepdims=True)
    return x * lax.rsqrt(ms + EPS) * w


def _silu(x):
    return x / (1.0 + jnp.exp(-x))


def _dot(a, b):
    return jnp.dot(a, b, preferred_element_type=F32)


def _dot_nt(a, b):
    return lax.dot_general(a, b, (((1,), (1,)), ((), ())), preferred_element_type=F32)


def _const_spec(shape):
    nd = len(shape)
    return pl.BlockSpec(shape, lambda *_: (0,) * nd)


def _memkv_kernel(mem_ref, nw_ref, wkv_ref, kv_ref):
    xn = _rms(mem_ref[0], nw_ref[...]).astype(BF16)
    kv_ref[0] = _dot(xn, wkv_ref[...]).astype(BF16)


def _memkv(mem, nw, wkv):
    b, m, d = mem.shape
    return pl.pallas_call(
        _memkv_kernel,
        grid=(b,),
        in_specs=[pl.BlockSpec((1, m, d), lambda i: (i, 0, 0)),
                  _const_spec((1, d)),
                  _const_spec((d, 2 * d))],
        out_specs=pl.BlockSpec((1, m, 2 * d), lambda i: (i, 0, 0)),
        out_shape=jax.ShapeDtypeStruct((b, m, 2 * d), BF16),
        compiler_params=_params(("parallel",)),
        name="memkv",
    )(mem, nw, wkv)


def _inproj_kernel(xp_ref, x_ref, xn_ref, nw_ref, win_ref, qnw_ref, kvnw_ref, wqm_ref, wqr_ref,
                   wk_ref, wv_ref, cos_ref, sin_ref, cw_ref, cb_ref,
                   z_ref, xbc_ref, dt_ref, q_ref, k_ref, v_ref, *, tiles_per_seq):
    t = x_ref.shape[0]
    pos = pl.program_id(0) % tiles_per_seq
    x_ext = jnp.concatenate([xp_ref[...], x_ref[...], xn_ref[...]], axis=0)
    hn = _rms(x_ext, nw_ref[...]).astype(BF16)
    proj_ext = _dot(hn, win_ref[...])
    proj = proj_ext[SUBLANE:SUBLANE + t, :]
    z_ref[...] = proj[:, O_Z:O_XBC]
    dt_ref[...] = proj[:, O_DT:D_IN_PAD]
    row = lax.broadcasted_iota(jnp.int32, (t + 2 * SUBLANE, 1), 0)
    inside = ((row >= SUBLANE) | (pos > 0)) & ((row < SUBLANE + t) | (pos < tiles_per_seq - 1))
    xbc_ext = jnp.where(inside, proj_ext[:, O_XBC:O_CQ], 0.0)
    acc = cb_ref[...] + jnp.zeros((t, CONV_CH), F32)
    n_ext = t + 2 * SUBLANE
    for j in range(D_CONV):
        tap = pltpu.roll(xbc_ext, (D_CONV // 2 - j) % n_ext, 0)[SUBLANE:SUBLANE + t, :]
        acc = acc + tap * cw_ref[j:j + 1, :]
    xbc_ref[...] = _silu(acc).astype(BF16)
    cqn = _rms(proj[:, O_CQ:O_CKV], qnw_ref[...]).astype(BF16)
    ckvn = _rms(proj[:, O_CKV:O_KA], kvnw_ref[...]).astype(BF16)
    cos_t = cos_ref[...]
    sin_t = sin_ref[...]
    qscale = LOG2E * (QK_NOPE + QK_ROPE) ** -0.5
    cos_q = cos_t * qscale
    sin_q = sin_t * qscale
    qm = _dot(cqn, wqm_ref[...])
    qr = _dot(cqn, wqr_ref[...])
    kr = proj[:, O_KA:O_KB] * cos_t + proj[:, O_KB:O_DT] * sin_t
    km = _dot(ckvn, wk_ref[...])
    vm = _dot(ckvn, wv_ref[...])
    lane = lax.broadcasted_iota(jnp.int32, (1, HEAD_PAD), 1)
    one_col = jnp.where(lane == V_ONE_LANE, 1.0, 0.0).astype(F32)
    for h in range(MLA_HEADS):
        sl = slice(h * HEAD_PAD, (h + 1) * HEAD_PAD)
        q_ref[:, sl] = (qm[:, sl] * cos_q + qr[:, sl] * sin_q).astype(BF16)
        k_ref[:, sl] = (km[:, sl] + kr).astype(BF16)
        v_ref[:, sl] = (vm[:, sl] + one_col).astype(BF16)


def _inproj(x2, nw, win, qnw, kvnw, wqm, wqr, wk, wv, cos_t, sin_t, cw, cb, seq):
    n, d = x2.shape
    t = TOKEN_TILE
    tiles_per_seq = seq // t
    rb = t // SUBLANE
    nrb = n // SUBLANE
    tok = lambda w: pl.BlockSpec((t, w), lambda i: (i, 0))
    halo_prev = pl.BlockSpec((SUBLANE, d), lambda i: (jnp.maximum(i * rb - 1, 0), 0))
    halo_next = pl.BlockSpec((SUBLANE, d), lambda i: (jnp.minimum(i * rb + rb, nrb - 1), 0))
    pos = pl.BlockSpec((t, HEAD_PAD), lambda i: (i % tiles_per_seq, 0))
    hp = MLA_HEADS * HEAD_PAD
    outs = [jax.ShapeDtypeStruct((n, SSM_D), F32), jax.ShapeDtypeStruct((n, CONV_CH), BF16),
            jax.ShapeDtypeStruct((n, LANE), F32), jax.ShapeDtypeStruct((n, hp), BF16),
            jax.ShapeDtypeStruct((n, hp), BF16), jax.ShapeDtypeStruct((n, hp), BF16)]
    return pl.pallas_call(
        functools.partial(_inproj_kernel, tiles_per_seq=tiles_per_seq),
        grid=(n // t,),
        in_specs=[halo_prev, tok(d), halo_next, _const_spec((1, d)), _const_spec((d, D_IN_PAD)),
                  _const_spec((1, Q_LORA)), _const_spec((1, KV_LORA)),
                  _const_spec((Q_LORA, hp)), _const_spec((Q_LORA, hp)),
                  _const_spec((KV_LORA, hp)), _const_spec((KV_LORA, hp)),
                  pos, pos, _const_spec((SUBLANE, CONV_CH)), _const_spec((1, CONV_CH))],
        out_specs=[tok(SSM_D), tok(CONV_CH), tok(LANE), tok(hp), tok(hp), tok(hp)],
        out_shape=outs,
        compiler_params=_params(("parallel",)),
        name="inproj",
    )(x2, x2, x2, nw, win, qnw, kvnw, wqm, wqr, wk, wv, cos_t, sin_t, cw, cb)


def _softplus(x):
    return jnp.maximum(x, 0.0) + jnp.log1p(jnp.exp(-jnp.abs(x)))


def _dot_pieces(a, passes, fn):
    out = None
    rem = a
    for _ in range(passes):
        piece = rem.astype(BF16)
        rem = rem - piece.astype(F32)
        term = fn(piece)
        out = term if out is None else out + term
    return out


def _ssd_chunk(xbc, dt_raw, dtb_ref, a_ref, e64_ref, h_in, reverse, lane_off):
    L = CHUNK
    P = SSM_HEAD_DIM
    gw = SSM_HPG * P
    xs = xbc[:, :SSM_D].astype(F32)
    bm = xbc[:, SSM_D:SSM_D + SSM_GROUPS * SSM_STATE]
    cm = xbc[:, SSM_D + SSM_GROUPS * SSM_STATE:]
    dt = _softplus(dt_raw + dtb_ref[...])
    dta = dt * a_ref[...]
    row = lax.broadcasted_iota(jnp.int32, (L, L), 0)
    col = lax.broadcasted_iota(jnp.int32, (L, L), 1)
    mask = (col >= row) if reverse else (col <= row)
    tri = jnp.where(mask, 1.0, 0.0).astype(BF16)
    cum = _dot_pieces(dta, 3, lambda piece: _dot(tri, piece))
    cum_t = cum.T
    dt_t = dt.T
    last = 0 if reverse else L - 1
    exp_cum = jnp.exp(cum)
    w_state = dt * jnp.exp(cum[last:last + 1, :] - cum)
    scal64 = _dot_pieces(jnp.concatenate([w_state, exp_cum], axis=0), 2,
                         lambda piece: _dot(piece, e64_ref[...]))
    xw = (xs * scal64[:L]).astype(BF16)
    exp_cum64 = scal64[L:]
    exp_total64 = exp_cum64[last:last + 1, :]
    lane = lax.broadcasted_iota(jnp.int32, (L, 2 * P), 1)
    ys = []
    h_out = []
    for g in range(SSM_GROUPS):
        bg = bm[:, g * SSM_STATE:(g + 1) * SSM_STATE]
        cg = cm[:, g * SSM_STATE:(g + 1) * SSM_STATE]
        cb = _dot_nt(cg, bg)
        h_prev = h_in[g]
        y_off = _dot(cg, h_prev.astype(BF16)) * exp_cum64[:, g * gw:(g + 1) * gw]
        pairs = []
        for pr in range(SSM_HPG // 2):
            x_pair = xbc[:, g * gw + pr * 2 * P:g * gw + (pr + 1) * 2 * P]
            halves = []
            for q in range(2):
                hh = g * SSM_HPG + pr * 2 + q
                ln = lane_off + hh
                seg = cum[:, ln:ln + 1] - cum_t[ln:ln + 1, :]
                dec = jnp.exp(jnp.where(mask, seg, -jnp.inf))
                m = (cb * dec * dt_t[ln:ln + 1, :]).astype(BF16)
                halves.append(_dot(m, x_pair))
            pairs.append(jnp.where(lane < P, halves[0], halves[1]))
        ys.append(y_off + jnp.concatenate(pairs, axis=1))
        st = _dot(bg.astype(F32).T.astype(BF16), xw[:, g * gw:(g + 1) * gw])
        h_out.append(h_prev * exp_total64[:, g * gw:(g + 1) * gw] + st)
    return jnp.concatenate(ys, axis=1), xs, h_out


def _ssd_kernel(xf_ref, xb_ref, dtf_ref, dtb_ref, bias_ref, a_ref, dskip_ref,
                e64f_ref, e64b_ref, yf_ref, yb_ref, hf_ref, hb_ref):
    @pl.when(pl.program_id(1) == 0)
    def _():
        hf_ref[...] = jnp.zeros_like(hf_ref)
        hb_ref[...] = jnp.zeros_like(hb_ref)

    n_sub = xf_ref.shape[1] // CHUNK
    h_f = [hf_ref[g] for g in range(SSM_GROUPS)]
    h_b = [hb_ref[g] for g in range(SSM_GROUPS)]
    for i in range(n_sub):
        rows = slice(i * CHUNK, (i + 1) * CHUNK)
        y_f, xs_f, h_f = _ssd_chunk(xf_ref[0, rows, :], dtf_ref[0, rows, :], bias_ref, a_ref,
                                    e64f_ref, h_f, False, 0)
        yf_ref[0, rows, :] = y_f + xs_f * dskip_ref[...]
        rows = slice((n_sub - 1 - i) * CHUNK, (n_sub - i) * CHUNK)
        y_b, _, h_b = _ssd_chunk(xb_ref[0, rows, :], dtb_ref[0, rows, :], bias_ref, a_ref,
                                 e64b_ref, h_b, True, SSM_HEADS)
        yb_ref[0, rows, :] = y_b
    for g in range(SSM_GROUPS):
        hf_ref[g] = h_f[g]
        hb_ref[g] = h_b[g]


def _head_lane_expander(lane_off, width):
    src_lane = lax.broadcasted_iota(jnp.int32, (LANE, SSM_HEADS * width), 0)
    dst_head = lax.broadcasted_iota(jnp.int32, (LANE, SSM_HEADS * width), 1) // width
    return (src_lane == dst_head + lane_off).astype(BF16)


def _ssd(xbc, dt, bias, a_row, dskip):
    b, s, _ = xbc.shape
    expanders = [_head_lane_expander(off, SSM_HEAD_DIM) for off in (0, SSM_HEADS)]
    rows = SSD_CHUNKS_PER_STEP * CHUNK
    nc = s // rows
    chunk = lambda w, f: pl.BlockSpec((1, rows, w), lambda i, c: (i, f(c), 0))
    fwd = lambda c: c
    bwd = lambda c: nc - 1 - c
    hshape = (SSM_GROUPS, SSM_STATE, SSM_HPG * SSM_HEAD_DIM)
    return pl.pallas_call(
        _ssd_kernel,
        grid=(b, nc),
        in_specs=[chunk(CONV_CH, fwd), chunk(CONV_CH, bwd), chunk(LANE, fwd), chunk(LANE, bwd),
                  _const_spec((1, LANE)), _const_spec((1, LANE)), _const_spec((1, SSM_D))]
        + [_const_spec(e.shape) for e in expanders],
        out_specs=[chunk(SSM_D, fwd), chunk(SSM_D, bwd)],
        out_shape=[jax.ShapeDtypeStruct((b, s, SSM_D), F32)] * 2,
        scratch_shapes=[pltpu.VMEM(hshape, F32), pltpu.VMEM(hshape, F32)],
        compiler_params=_params(("parallel", "arbitrary")),
        name="ssd",
    )(xbc, xbc, dt, dt, bias, a_row, dskip, *expanders)


def _flash_kernel(q_ref, k_ref, v_ref, o_ref, *, tk):
    s_len = k_ref.shape[1]
    tq = q_ref.shape[1]
    sls = [slice(hh * HEAD_PAD, (hh + 1) * HEAD_PAD) for hh in range(2)]

    def body(j, carry):
        off = pl.multiple_of(j * tk, tk)
        new = []
        for sl, (m, acc) in zip(sls, carry):
            kj = k_ref[0, pl.ds(off, tk), sl]
            vj = v_ref[0, pl.ds(off, tk), sl]
            s = _dot_nt(q_ref[0, :, sl], kj)
            m_new = jnp.maximum(m, jnp.max(s, axis=1, keepdims=True))
            alpha = jnp.exp2(m - m_new)
            p = jnp.exp2(s - m_new).astype(BF16)
            new.append((m_new, acc * alpha + _dot(p, vj)))
        return tuple(new)

    m0 = jnp.full((tq, 1), -jnp.inf, F32)
    acc0 = jnp.zeros((tq, HEAD_PAD), F32)
    carry = lax.fori_loop(0, s_len // tk, body, ((m0, acc0), (m0, acc0)), unroll=ATTN_UNROLL)
    outs = [acc / acc[:, V_ONE_LANE:V_ONE_LANE + 1] for _, acc in carry]
    lane = lax.broadcasted_iota(jnp.int32, (tq, HEAD_PAD), 1)
    o_ref[0] = jnp.where(lane < V_DIM, outs[0], pltpu.roll(outs[1], V_DIM, 1)).astype(BF16)


def _flash(q, k, v):
    b, s, hp = q.shape
    tq = min(ATTN_TQ, s)
    tk = min(ATTN_TK, s)
    pairs = MLA_HEADS // 2
    pw = 2 * HEAD_PAD
    return pl.pallas_call(
        functools.partial(_flash_kernel, tk=tk),
        grid=(b, pairs, s // tq),
        in_specs=[pl.BlockSpec((1, tq, pw), lambda i, p, j: (i, j, p)),
                  pl.BlockSpec((1, s, pw), lambda i, p, j: (i, 0, p)),
                  pl.BlockSpec((1, s, pw), lambda i, p, j: (i, 0, p))],
        out_specs=pl.BlockSpec((1, tq, 2 * V_DIM), lambda i, p, j: (i, j, p)),
        out_shape=jax.ShapeDtypeStruct((b, s, MLA_D), BF16),
        compiler_params=_params(("parallel", "parallel", "arbitrary")),
        name="mla_flash",
    )(q, k, v)


def _cross_attn(h1, kv_ref, pre_w, wq_ref, wo_ref, post_w):
    hn = _rms(h1, pre_w).astype(BF16)
    q = (_dot(hn, wq_ref[...]) * (XA_HEAD_DIM ** -0.5)).astype(BF16)
    heads = []
    for hd in range(XA_HEADS):
        sl = slice(hd * XA_HEAD_DIM, (hd + 1) * XA_HEAD_DIM)
        kh = kv_ref[0, :, sl]
        vh = kv_ref[0, :, D_MODEL + hd * XA_HEAD_DIM:D_MODEL + (hd + 1) * XA_HEAD_DIM]
        s = _dot_nt(q[:, sl], kh)
        p = jnp.exp(s - jnp.max(s, axis=1, keepdims=True))
        l = jnp.sum(p, axis=1, keepdims=True)
        heads.append((_dot(p.astype(BF16), vh) / l).astype(BF16))
    o = jnp.concatenate(heads, axis=1)
    xa = _dot(o, wo_ref[...])
    return h1 + _rms(xa, post_w)


def _postmix_even_kernel(h_ref, yf_ref, yb_ref, z_ref, o_ref, snw_ref, wout_ref, mpost_ref,
                         kv_ref, xpre_ref, wq_ref, wo_ref, xpost_ref, out_ref):
    y = (yf_ref[0] + yb_ref[0]) * _silu(z_ref[0])
    gw = SSM_D // SSM_GROUPS
    parts = []
    for g in range(SSM_GROUPS):
        yg = y[:, g * gw:(g + 1) * gw]
        parts.append(yg * lax.rsqrt(jnp.mean(yg * yg, axis=-1, keepdims=True) + EPS))
    y_ssd = (jnp.concatenate(parts, axis=1) * snw_ref[...]).astype(BF16)
    mix = _dot(y_ssd, wout_ref[:SSM_D, :]) + _dot(o_ref[0], wout_ref[SSM_D:, :])
    h1 = h_ref[0] + _rms(mix, mpost_ref[...])
    out_ref[0] = _cross_attn(h1, kv_ref, xpre_ref[...], wq_ref, wo_ref, xpost_ref[...])


def _postmix_odd_kernel(h_ref, f_ref, wmix_ref, mpost_ref,
                        kv_ref, xpre_ref, wq_ref, wo_ref, xpost_ref, out_ref):
    mix = _dot(f_ref[0], wmix_ref[...])
    h1 = h_ref[0] + _rms(mix, mpost_ref[...])
    out_ref[0] = _cross_attn(h1, kv_ref, xpre_ref[...], wq_ref, wo_ref, xpost_ref[...])


def _postmix(kernel, h, mixed, mixed_w, consts_a, kv, kv_off, consts_b):
    b, s, d = h.shape
    t = TOKEN_TILE
    tok = lambda w: pl.BlockSpec((1, t, w), lambda i, j: (i, j, 0))
    in_specs = [tok(d)] + [tok(w) for w in mixed_w]
    in_specs += [_const_spec(c.shape) for c in consts_a]
    in_specs += [pl.BlockSpec((1, N_MEM, 2 * d), lambda i, j: (i + kv_off, 0, 0))]
    in_specs += [_const_spec(c.shape) for c in consts_b]
    return pl.pallas_call(
        kernel,
        grid=(b, s // t),
        in_specs=in_specs,
        out_specs=tok(d),
        out_shape=jax.ShapeDtypeStruct((b, s, d), F32),
        compiler_params=_params(("parallel", "parallel")),
        name="postmix",
    )(h, *mixed, *consts_a, kv, *consts_b)


def _ffn_kernel(h_ref, pre_ref, wg_ref, wu_ref, wd_ref, post_ref, *rest):
    out_ref = rest[-2] if len(rest) == 3 else rest[0]
    h = h_ref[...]
    hn = _rms(h, pre_ref[...]).astype(BF16)
    d_ff = wg_ref.shape[1]
    acc = jnp.zeros(h.shape, F32)
    for c in range(d_ff // FFN_CHUNK):
        sl = slice(c * FFN_CHUNK, (c + 1) * FFN_CHUNK)
        g = _dot(hn, wg_ref[:, sl])
        u = _dot(hn, wu_ref[:, sl])
        acc = acc + _dot((_silu(g) * u).astype(BF16), wd_ref[sl, :])
    out = h + _rms(acc, post_ref[...])
    out_ref[...] = out
    if len(rest) == 3:
        rest[2][...] = _rms(out, rest[0][...]).astype(BF16)


def _ffn(h2, pre, wg, wu, wd, post, next_pre=None):
    n, d = h2.shape
    t = TOKEN_TILE
    tok = pl.BlockSpec((t, d), lambda i: (i, 0))
    in_specs = [tok, _const_spec((1, d)), _const_spec(wg.shape), _const_spec(wu.shape),
                _const_spec(wd.shape), _const_spec((1, d))]
    args = [h2, pre, wg, wu, wd, post]
    out_specs, out_shape = tok, jax.ShapeDtypeStruct((n, d), F32)
    if next_pre is not None:
        in_specs.append(_const_spec((1, d)))
        args.append(next_pre)
        out_specs, out_shape = [tok, tok], [out_shape, jax.ShapeDtypeStruct((n, d), BF16)]
    return pl.pallas_call(
        _ffn_kernel,
        grid=(n // t,),
        in_specs=in_specs,
        out_specs=out_specs,
        out_shape=out_shape,
        compiler_params=_params(("parallel",)),
        name="ffn",
    )(*args)


def _fnet_a_kernel(x_ref, cs_ref, m1_ref, a_ref):
    gd = FOURIER_GROUP_DIM
    n1 = x_ref.shape[1]
    n_col = x_ref.shape[2] // D_MODEL
    slabs = [x_ref[0, :, c * D_MODEL + g * gd:c * D_MODEL + (g + 1) * gd]
             for c in range(n_col) for g in range(FOURIER_GROUPS)]
    y = _dot(jnp.concatenate(slabs, axis=0), cs_ref[...])
    for c in range(n_col):
        rows = [slice((c * FOURIER_GROUPS + g) * n1, (c * FOURIER_GROUPS + g + 1) * n1)
                for g in range(FOURIER_GROUPS)]
        yr = jnp.concatenate([y[r, :gd] for r in rows], axis=1)
        yi = jnp.concatenate([y[r, gd:] for r in rows], axis=1)
        stack = jnp.concatenate([yr, yi], axis=0).astype(BF16)
        a_ref[0, :, c * D_MODEL:(c + 1) * D_MODEL] = _dot(m1_ref[...], stack).astype(BF16)


def _fnet_b_kernel(a_ref, g_ref, f_ref):
    for c in range(g_ref.shape[0]):
        stack = jnp.concatenate([a_ref[0, 0, c], a_ref[0, 1, c]], axis=0)
        f_ref[0, :, c * D_MODEL:(c + 1) * D_MODEL] = _dot(g_ref[c], stack).astype(BF16)


def _fnet(xn, cs, m1, gt):
    b, s, d = xn.shape
    n1 = m1.shape[0] // 2
    n2 = s // n1
    ca = min(FNET_COLS, n2)
    cb = min(FNET_COLS, n1)
    a = pl.pallas_call(
        _fnet_a_kernel,
        grid=(b, n2 // ca),
        in_specs=[pl.BlockSpec((1, n1, ca * d), lambda i, j: (i, 0, j)),
                  _const_spec(cs.shape), _const_spec(m1.shape)],
        out_specs=pl.BlockSpec((1, 2 * n1, ca * d), lambda i, j: (i, 0, j)),
        out_shape=jax.ShapeDtypeStruct((b, 2 * n1, n2 * d), BF16),
        compiler_params=_params(("parallel", "parallel")),
        name="fnet_a",
    )(xn.reshape(b, n1, n2 * d), cs, m1)
    f = pl.pallas_call(
        _fnet_b_kernel,
        grid=(b, n1 // cb),
        in_specs=[pl.BlockSpec((1, 2, cb, n2, d), lambda i, j: (i, 0, j, 0, 0)),
                  pl.BlockSpec((cb, n2, 2 * n2), lambda i, j: (j, 0, 0))],
        out_specs=pl.BlockSpec((1, n2, cb * d), lambda i, j: (i, 0, j)),
        out_shape=jax.ShapeDtypeStruct((b, n2, n1 * d), BF16),
        compiler_params=_params(("parallel", "parallel")),
        name="fnet_b",
    )(a.reshape(b, 2, n1, n2, d), gt)
    return f.reshape(b, s, d)


def _rope_tables(s):
    inv = ROPE_THETA ** (-jnp.arange(0, QK_ROPE, 2, dtype=F32) / QK_ROPE)
    ang = jnp.arange(s, dtype=F32)[:, None] * inv[None, :]
    cos2 = jnp.concatenate([jnp.cos(ang), jnp.cos(ang)], axis=1)
    sin2 = jnp.concatenate([jnp.sin(ang), jnp.sin(ang)], axis=1)
    pad = HEAD_PAD - QK_NOPE - QK_ROPE
    cos_t = jnp.concatenate([jnp.ones((s, QK_NOPE), F32), cos2, jnp.ones((s, pad), F32)], axis=1)
    sin_t = jnp.concatenate([jnp.zeros((s, QK_NOPE), F32), sin2, jnp.zeros((s, pad), F32)], axis=1)
    return cos_t, sin_t


def _rot_cols(w):
    half = w.shape[-1] // 2
    return jnp.concatenate([-w[..., half:], w[..., :half]], axis=-1)


def _pad_cols(w, left, total):
    return jnp.pad(w, ((0, 0), (left, total - left - w.shape[1])))


def _even_weights(w_in, w_uq, w_ukv):
    o1 = SSM_D
    o2 = o1 + CONV_CH
    o3 = o2 + 2 * SSM_HEADS
    o4 = o3 + Q_LORA
    o5 = o4 + KV_LORA
    w_z, w_xbc, w_dt, w_cq, w_ckv, w_kr = (w_in[:, :o1], w_in[:, o1:o2], w_in[:, o2:o3],
                                             w_in[:, o3:o4], w_in[:, o4:o5], w_in[:, o5:])
    win = jnp.concatenate([
        w_z, w_xbc, w_cq, w_ckv,
        _pad_cols(w_kr, QK_NOPE, LANE), _pad_cols(_rot_cols(w_kr), QK_NOPE, LANE),
        _pad_cols(w_dt, 0, LANE)], axis=1).astype(BF16)
    dq = QK_NOPE + QK_ROPE
    wq = w_uq.reshape(Q_LORA, MLA_HEADS, dq)
    zq = jnp.zeros((Q_LORA, MLA_HEADS, HEAD_PAD - dq), F32)
    wqm = jnp.concatenate([wq, zq], axis=-1).reshape(Q_LORA, -1).astype(BF16)
    wqr = jnp.concatenate([jnp.zeros((Q_LORA, MLA_HEADS, QK_NOPE), F32),
                           _rot_cols(wq[..., QK_NOPE:]), zq], axis=-1).reshape(Q_LORA, -1).astype(BF16)
    wkv = w_ukv.reshape(KV_LORA, MLA_HEADS, QK_NOPE + V_DIM)
    zk = jnp.zeros((KV_LORA, MLA_HEADS, HEAD_PAD - QK_NOPE), F32)
    wk = jnp.concatenate([wkv[..., :QK_NOPE], zk], axis=-1).reshape(KV_LORA, -1).astype(BF16)
    zv = jnp.zeros((KV_LORA, MLA_HEADS, HEAD_PAD - V_DIM), F32)
    wv = jnp.concatenate([wkv[..., QK_NOPE:], zv], axis=-1).reshape(KV_LORA, -1).astype(BF16)
    return win, wqm, wqr, wk, wv


def _fnet_tables(s):
    n2 = CHUNK
    n1 = s // n2
    gd = FOURIER_GROUP_DIM
    ci = jnp.arange(gd, dtype=jnp.int32)
    ang_c = (2.0 * math.pi / gd) * ((ci[:, None] * ci[None, :]) % gd).astype(F32)
    cs = (jnp.concatenate([jnp.cos(ang_c), -jnp.sin(ang_c)], axis=1) * gd ** -0.5).astype(BF16)
    i1 = jnp.arange(n1, dtype=jnp.int32)
    ang1 = (2.0 * math.pi / n1) * ((i1[:, None] * i1[None, :]) % n1).astype(F32)
    c1, s1 = jnp.cos(ang1), jnp.sin(ang1)
    m1 = jnp.concatenate([jnp.concatenate([c1, s1], axis=1),
                          jnp.concatenate([-s1, c1], axis=1)], axis=0).astype(BF16)
    k1 = jnp.arange(n1, dtype=jnp.int32)[:, None, None]
    k2 = jnp.arange(n2, dtype=jnp.int32)[None, :, None]
    j2 = jnp.arange(n2, dtype=jnp.int32)[None, None, :]
    ang = (2.0 * math.pi / s) * ((j2 * (k1 + n1 * k2)) % s).astype(F32)
    gt = (jnp.concatenate([jnp.cos(ang), jnp.sin(ang)], axis=2) * s ** -0.5).astype(BF16)
    return cs, m1, gt


def _row(v, width=None):
    v = v.astype(F32).reshape(1, -1)
    if width is not None:
        v = jnp.pad(v, ((0, 0), (0, width - v.shape[1])))
    return v


def _trunk(x, kv_layers, kv_off, p):
    b, s, d = x.shape
    n = b * s
    h = x
    assert d == D_MODEL and s % min(ATTN_TQ, s) == 0 and s % min(ATTN_TK, s) == 0
    assert s % TOKEN_TILE == 0 and s % (SSD_CHUNKS_PER_STEP * CHUNK) == 0 and (s // CHUNK) % SUBLANE == 0
    z, xbc, dt, q, k, v = _inproj(h.reshape(n, d), p["mix_pre"][0], p["win"], p["q_norm"], p["kv_norm"],
                                  p["wqm"], p["wqr"], p["wk"], p["wv"], p["cos"][:s], p["sin"][:s],
                                  p["conv_w"], p["conv_b"], s)
    yf, yb = _ssd(xbc.reshape(b, s, -1), dt.reshape(b, s, -1), p["dt_bias"], p["a_row"], p["d_skip"])
    hp = MLA_HEADS * HEAD_PAD
    o = _flash(q.reshape(b, s, hp), k.reshape(b, s, hp), v.reshape(b, s, hp))
    h = _postmix(_postmix_even_kernel, h, (yf, yb, z.reshape(b, s, -1), o),
                 (SSM_D, SSM_D, SSM_D, MLA_D),
                 (p["ssm_norm"], p["w_out"], p["mix_post"][0]), kv_layers[0], kv_off,
                 (p["xa_pre"][0], p["xa_wq"][0], p["xa_wo"][0], p["xa_post"][0]))
    h, hn = _ffn(h.reshape(n, d), p["ffn_pre"][0], p["wg"][0], p["wu"][0], p["wd"][0],
                 p["ffn_post"][0], next_pre=p["mix_pre"][1])
    h = h.reshape(b, s, d)
    cs, m1, gt = _fnet_tables(s)
    f = _fnet(hn.reshape(b, s, d), cs, m1, gt)
    h = _postmix(_postmix_odd_kernel, h, (f,), (d,), (p["w_mix"], p["mix_post"][1]),
                 kv_layers[1], kv_off,
                 (p["xa_pre"][1], p["xa_wq"][1], p["xa_wo"][1], p["xa_post"][1]))
    h = _ffn(h.reshape(n, d), p["ffn_pre"][1], p["wg"][1], p["wu"][1], p["wd"][1],
             p["ffn_post"][1]).reshape(b, s, d)
    return h


def kernel(x_prompt, x_sample, mem_prompt, mem_sample, norm_mix_pre, norm_mix_post, norm_xa_pre, norm_xa_post, norm_mem, xa_wq, xa_wkv, xa_wo, norm_ffn_pre, norm_ffn_post, ffn_w_gu, ffn_w_down, ev_w_in, ev_conv_w, ev_conv_b, ev_a_log_f, ev_a_log_b, ev_dt_bias_f, ev_dt_bias_b, ev_d_skip, ev_ssm_norm, ev_q_norm, ev_w_uq, ev_kv_norm, ev_w_ukv, ev_w_out, od_w_mix):
    depth = norm_mix_pre.shape[0]
    assert depth == 2 and ev_w_in.shape[0] == 1 and od_w_mix.shape[0] == 1
    d_ff = ffn_w_down.shape[1]
    assert d_ff % FFN_CHUNK == 0
    s_max = max(x_prompt.shape[1], x_sample.shape[1])
    cos_t, sin_t = _rope_tables(s_max)
    win, wqm, wqr, wk, wv = _even_weights(ev_w_in[0], ev_w_uq[0], ev_w_ukv[0])
    rows = lambda w: [_row(w[i]) for i in range(depth)]
    p = {
        "mix_pre": rows(norm_mix_pre), "mix_post": rows(norm_mix_post),
        "xa_pre": rows(norm_xa_pre), "xa_post": rows(norm_xa_post),
        "ffn_pre": rows(norm_ffn_pre), "ffn_post": rows(norm_ffn_post),
        "xa_wq": [xa_wq[i].astype(BF16) for i in range(depth)],
        "xa_wo": [xa_wo[i].astype(BF16) for i in range(depth)],
        "wg": [ffn_w_gu[i, :, :d_ff].astype(BF16) for i in range(depth)],
        "wu": [ffn_w_gu[i, :, d_ff:].astype(BF16) for i in range(depth)],
        "wd": [ffn_w_down[i].astype(BF16) for i in range(depth)],
        "win": win, "wqm": wqm, "wqr": wqr, "wk": wk, "wv": wv,
        "q_norm": _row(ev_q_norm[0]), "kv_norm": _row(ev_kv_norm[0]),
        "cos": cos_t, "sin": sin_t,
        "conv_w": jnp.pad(ev_conv_w[0].astype(F32), ((0, SUBLANE - D_CONV), (0, 0))),
        "conv_b": _row(ev_conv_b[0]),
        "dt_bias": _row(jnp.concatenate([ev_dt_bias_f[0], ev_dt_bias_b[0]]), LANE),
        "a_row": _row(-jnp.exp(jnp.concatenate([ev_a_log_f[0], ev_a_log_b[0]]).astype(F32)), LANE),
        "d_skip": _row(jnp.repeat(ev_d_skip[0].astype(F32), SSM_HEAD_DIM)),
        "ssm_norm": _row(ev_ssm_norm[0]),
        "w_out": ev_w_out[0].astype(BF16),
        "w_mix": od_w_mix[0].astype(BF16),
    }
    mem = jnp.concatenate([mem_prompt, mem_sample], axis=0)
    kv_layers = [_memkv(mem, _row(norm_mem[i]), xa_wkv[i].astype(BF16)) for i in range(depth)]
    y_prompt = _trunk(x_prompt, kv_layers, 0, p)
    y_sample = _trunk(x_sample, kv_layers, x_prompt.shape[0], p)
    return (y_prompt, y_sample)
```

```python
import functools
import math

import jax
import jax.numpy as jnp
from jax import lax
from jax.experimental import pallas as pl
from jax.experimental.pallas import tpu as pltpu

F32 = jnp.float32
BF16 = jnp.bfloat16

EPS = 1e-6
D_MODEL = 1024
N_MEM = 256

SSM_HEADS = 8
SSM_HEAD_DIM = 64
SSM_D = SSM_HEADS * SSM_HEAD_DIM
SSM_GROUPS = 2
SSM_HPG = SSM_HEADS // SSM_GROUPS
SSM_STATE = 128
D_CONV = 5
CONV_CH = SSM_D + 2 * SSM_GROUPS * SSM_STATE
CHUNK = 128

MLA_HEADS = 8
QK_NOPE = 64
QK_ROPE = 32
V_DIM = 64
Q_LORA = 256
KV_LORA = 128
ROPE_THETA = 10000.0
MLA_D = MLA_HEADS * V_DIM

FOURIER_GROUPS = 4
FOURIER_GROUP_DIM = D_MODEL // FOURIER_GROUPS

XA_HEADS = 4
XA_HEAD_DIM = D_MODEL // XA_HEADS
XA_SCALE = XA_HEAD_DIM ** -0.5
assert math.frexp(XA_SCALE)[0] == 0.5

LANE = 128
SUBLANE = 8
HEAD_PAD = LANE
V_ONE_LANE = V_DIM
VMEM_LIMIT = 56 * 1024 * 1024

TOKEN_TILE = 512
FFN_CHUNK = 256
ATTN_TQ = 1024
ATTN_TK = 2048
ATTN_UNROLL = 4
FNET_COLS = 8
SSD_CHUNKS_PER_STEP = 4
LOG2E = 1.4426950408889634

O_Z = 0
O_XBC = O_Z + SSM_D
O_CQ = O_XBC + CONV_CH
O_CKV = O_CQ + Q_LORA
O_KA = O_CKV + KV_LORA
O_KB = O_KA + LANE
O_DT = O_KB + LANE
D_IN_PAD = O_DT + LANE


def _params(sem, vmem=VMEM_LIMIT):
    return pltpu.CompilerParams(dimension_semantics=sem, vmem_limit_bytes=vmem)


def _rms(x, w):
    ms = jnp.mean(x * x, axis=-1, keepdims=True)
    return x * lax.rsqrt(ms + EPS) * w


def _silu(x):
    return x / (1.0 + jnp.exp(-x))


def _dot(a, b):
    return jnp.dot(a, b, preferred_element_type=F32)


def _dot_nt(a, b):
    return lax.dot_general(a, b, (((1,), (1,)), ((), ())), preferred_element_type=F32)


def _const_spec(shape):
    nd = len(shape)
    return pl.BlockSpec(shape, lambda *_: (0,) * nd)


def _memkv_kernel(mem_ref, nw_ref, wkv_ref, kv_ref):
    xn = _rms(mem_ref[0], nw_ref[...]).astype(BF16)
    kv_ref[0] = _dot(xn, wkv_ref[...]).astype(BF16)


def _memkv(mem, nw, wkv):
    b, m, d = mem.shape
    return pl.pallas_call(
        _memkv_kernel,
        grid=(b,),
        in_specs=[pl.BlockSpec((1, m, d), lambda i: (i, 0, 0)),
                  _const_spec((1, d)),
                  _const_spec((d, 2 * d))],
        out_specs=pl.BlockSpec((1, m, 2 * d), lambda i: (i, 0, 0)),
        out_shape=jax.ShapeDtypeStruct((b, m, 2 * d), BF16),
        compiler_params=_params(("parallel",)),
        name="memkv",
    )(mem, nw, wkv)


def _inproj_kernel(xp_ref, x_ref, xn_ref, nw_ref, win_ref, qnw_ref, kvnw_ref, wqm_ref, wqr_ref,
                   wk_ref, wv_ref, cos_ref, sin_ref, cw_ref, cb_ref,
                   z_ref, xbc_ref, dt_ref, q_ref, k_ref, v_ref, *, tiles_per_seq):
    t = x_ref.shape[0]
    pos = pl.program_id(0) % tiles_per_seq
    x_ext = jnp.concatenate([xp_ref[...], x_ref[...], xn_ref[...]], axis=0)
    hn = _rms(x_ext, nw_ref[...]).astype(BF16)
    proj_ext = _dot(hn, win_ref[...])
    proj = proj_ext[SUBLANE:SUBLANE + t, :]
    z_ref[...] = proj[:, O_Z:O_XBC]
    dt_ref[...] = proj[:, O_DT:D_IN_PAD]
    row = lax.broadcasted_iota(jnp.int32, (t + 2 * SUBLANE, 1), 0)
    inside = ((row >= SUBLANE) | (pos > 0)) & ((row < SUBLANE + t) | (pos < tiles_per_seq - 1))
    xbc_ext = jnp.where(inside, proj_ext[:, O_XBC:O_CQ], 0.0)
    acc = cb_ref[...] + jnp.zeros((t, CONV_CH), F32)
    n_ext = t + 2 * SUBLANE
    for j in range(D_CONV):
        tap = pltpu.roll(xbc_ext, (D_CONV // 2 - j) % n_ext, 0)[SUBLANE:SUBLANE + t, :]
        acc = acc + tap * cw_ref[j:j + 1, :]
    xbc_ref[...] = _silu(acc).astype(BF16)
    cqn = _rms(proj[:, O_CQ:O_CKV], qnw_ref[...]).astype(BF16)
    ckvn = _rms(proj[:, O_CKV:O_KA], kvnw_ref[...]).astype(BF16)
    cos_t = cos_ref[...]
    sin_t = sin_ref[...]
    qscale = LOG2E * (QK_NOPE + QK_ROPE) ** -0.5
    cos_q = cos_t * qscale
    sin_q = sin_t * qscale
    qm = _dot(cqn, wqm_ref[...])
    qr = _dot(cqn, wqr_ref[...])
    kr = proj[:, O_KA:O_KB] * cos_t + proj[:, O_KB:O_DT] * sin_t
    km = _dot(ckvn, wk_ref[...])
    vm = _dot(ckvn, wv_ref[...])
    lane = lax.broadcasted_iota(jnp.int32, (1, HEAD_PAD), 1)
    one_col = jnp.where(lane == V_ONE_LANE, 1.0, 0.0).astype(F32)
    for h in range(MLA_HEADS):
        sl = slice(h * HEAD_PAD, (h + 1) * HEAD_PAD)
        q_ref[:, sl] = (qm[:, sl] * cos_q + qr[:, sl] * sin_q).astype(BF16)
        k_ref[:, sl] = (km[:, sl] + kr).astype(BF16)
        v_ref[:, sl] = (vm[:, sl] + one_col).astype(BF16)


def _inproj(x2, nw, win, qnw, kvnw, wqm, wqr, wk, wv, cos_t, sin_t, cw, cb, seq):
    n, d = x2.shape
    t = TOKEN_TILE
    tiles_per_seq = seq // t
    rb = t // SUBLANE
    nrb = n // SUBLANE
    tok = lambda w: pl.BlockSpec((t, w), lambda i: (i, 0))
    halo_prev = pl.BlockSpec((SUBLANE, d), lambda i: (jnp.maximum(i * rb - 1, 0), 0))
    halo_next = pl.BlockSpec((SUBLANE, d), lambda i: (jnp.minimum(i * rb + rb, nrb - 1), 0))
    pos = pl.BlockSpec((t, HEAD_PAD), lambda i: (i % tiles_per_seq, 0))
    hp = MLA_HEADS * HEAD_PAD
    outs = [jax.ShapeDtypeStruct((n, SSM_D), F32), jax.ShapeDtypeStruct((n, CONV_CH), BF16),
            jax.ShapeDtypeStruct((n, LANE), F32), jax.ShapeDtypeStruct((n, hp), BF16),
            jax.ShapeDtypeStruct((n, hp), BF16), jax.ShapeDtypeStruct((n, hp), BF16)]
    return pl.pallas_call(
        functools.partial(_inproj_kernel, tiles_per_seq=tiles_per_seq),
        grid=(n // t,),
        in_specs=[halo_prev, tok(d), halo_next, _const_spec((1, d)), _const_spec((d, D_IN_PAD)),
                  _const_spec((1, Q_LORA)), _const_spec((1, KV_LORA)),
                  _const_spec((Q_LORA, hp)), _const_spec((Q_LORA, hp)),
                  _const_spec((KV_LORA, hp)), _const_spec((KV_LORA, hp)),
                  pos, pos, _const_spec((SUBLANE, CONV_CH)), _const_spec((1, CONV_CH))],
        out_specs=[tok(SSM_D), tok(CONV_CH), tok(LANE), tok(hp), tok(hp), tok(hp)],
        out_shape=outs,
        compiler_params=_params(("parallel",)),
        name="inproj",
    )(x2, x2, x2, nw, win, qnw, kvnw, wqm, wqr, wk, wv, cos_t, sin_t, cw, cb)


def _softplus(x):
    return jnp.maximum(x, 0.0) + jnp.log1p(jnp.exp(-jnp.abs(x)))


def _dot_pieces(a, passes, fn):
    out = None
    rem = a
    for _ in range(passes):
        piece = rem.astype(BF16)
        rem = rem - piece.astype(F32)
        term = fn(piece)
        out = term if out is None else out + term
    return out


def _ssd_chunk(xbc, dt_raw, dtb_ref, a_ref, e64_ref, h_in, reverse, lane_off):
    L = CHUNK
    P = SSM_HEAD_DIM
    gw = SSM_HPG * P
    xs = xbc[:, :SSM_D].astype(F32)
    bm = xbc[:, SSM_D:SSM_D + SSM_GROUPS * SSM_STATE]
    cm = xbc[:, SSM_D + SSM_GROUPS * SSM_STATE:]
    dt = _softplus(dt_raw + dtb_ref[...])
    dta = dt * a_ref[...]
    row = lax.broadcasted_iota(jnp.int32, (L, L), 0)
    col = lax.broadcasted_iota(jnp.int32, (L, L), 1)
    mask = (col >= row) if reverse else (col <= row)
    tri = jnp.where(mask, 1.0, 0.0).astype(BF16)
    cum = _dot_pieces(dta, 3, lambda piece: _dot(tri, piece))
    cum_t = cum.T
    dt_t = dt.T
    last = 0 if reverse else L - 1
    exp_cum = jnp.exp(cum)
    w_state = dt * jnp.exp(cum[last:last + 1, :] - cum)
    scal64 = _dot_pieces(jnp.concatenate([w_state, exp_cum], axis=0), 2,
                         lambda piece: _dot(piece, e64_ref[...]))
    xw = (xs * scal64[:L]).astype(BF16)
    exp_cum64 = scal64[L:]
    exp_total64 = exp_cum64[last:last + 1, :]
    lane = lax.broadcasted_iota(jnp.int32, (L, 2 * P), 1)
    ys = []
    h_out = []
    for g in range(SSM_GROUPS):
        bg = bm[:, g * SSM_STATE:(g + 1) * SSM_STATE]
        cg = cm[:, g * SSM_STATE:(g + 1) * SSM_STATE]
        cb = _dot_nt(cg, bg)
        h_prev = h_in[g]
        y_off = _dot(cg, h_prev.astype(BF16)) * exp_cum64[:, g * gw:(g + 1) * gw]
        pairs = []
        for pr in range(SSM_HPG // 2):
            x_pair = xbc[:, g * gw + pr * 2 * P:g * gw + (pr + 1) * 2 * P]
            ms = []
            for q in range(2):
                hh = g * SSM_HPG + pr * 2 + q
                ln = lane_off + hh
                seg = cum[:, ln:ln + 1] - cum_t[ln:ln + 1, :]
                dec = jnp.exp(jnp.where(mask, seg, -jnp.inf))
                ms.append((cb * dec * dt_t[ln:ln + 1, :]).astype(BF16))
            y2 = _dot(jnp.concatenate(ms, axis=0), x_pair)
            pairs.append(jnp.where(lane < P, y2[:L], y2[L:]))
        ys.append(y_off + jnp.concatenate(pairs, axis=1))
        st = _dot(bg.astype(F32).T.astype(BF16), xw[:, g * gw:(g + 1) * gw])
        h_out.append(h_prev * exp_total64[:, g * gw:(g + 1) * gw] + st)
    return jnp.concatenate(ys, axis=1), xs, h_out


def _ssd_kernel(xf_ref, xb_ref, dtf_ref, dtb_ref, bias_ref, a_ref, dskip_ref,
                e64f_ref, e64b_ref, yf_ref, yb_ref, hf_ref, hb_ref):
    @pl.when(pl.program_id(1) == 0)
    def _():
        hf_ref[...] = jnp.zeros_like(hf_ref)
        hb_ref[...] = jnp.zeros_like(hb_ref)

    n_sub = xf_ref.shape[1] // CHUNK
    h_f = [hf_ref[g] for g in range(SSM_GROUPS)]
    h_b = [hb_ref[g] for g in range(SSM_GROUPS)]
    for i in range(n_sub):
        rows = slice(i * CHUNK, (i + 1) * CHUNK)
        y_f, xs_f, h_f = _ssd_chunk(xf_ref[0, rows, :], dtf_ref[0, rows, :], bias_ref, a_ref,
                                    e64f_ref, h_f, False, 0)
        yf_ref[0, rows, :] = y_f + xs_f * dskip_ref[...]
        rows = slice((n_sub - 1 - i) * CHUNK, (n_sub - i) * CHUNK)
        y_b, _, h_b = _ssd_chunk(xb_ref[0, rows, :], dtb_ref[0, rows, :], bias_ref, a_ref,
                                 e64b_ref, h_b, True, SSM_HEADS)
        yb_ref[0, rows, :] = y_b
    for g in range(SSM_GROUPS):
        hf_ref[g] = h_f[g]
        hb_ref[g] = h_b[g]


def _head_lane_expander(lane_off, width):
    src_lane = lax.broadcasted_iota(jnp.int32, (LANE, SSM_HEADS * width), 0)
    dst_head = lax.broadcasted_iota(jnp.int32, (LANE, SSM_HEADS * width), 1) // width
    return (src_lane == dst_head + lane_off).astype(BF16)


def _ssd(xbc, dt, bias, a_row, dskip):
    b, s, _ = xbc.shape
    expanders = [_head_lane_expander(off, SSM_HEAD_DIM) for off in (0, SSM_HEADS)]
    rows = SSD_CHUNKS_PER_STEP * CHUNK
    nc = s // rows
    chunk = lambda w, f: pl.BlockSpec((1, rows, w), lambda i, c: (i, f(c), 0))
    fwd = lambda c: c
    bwd = lambda c: nc - 1 - c
    hshape = (SSM_GROUPS, SSM_STATE, SSM_HPG * SSM_HEAD_DIM)
    return pl.pallas_call(
        _ssd_kernel,
        grid=(b, nc),
        in_specs=[chunk(CONV_CH, fwd), chunk(CONV_CH, bwd), chunk(LANE, fwd), chunk(LANE, bwd),
                  _const_spec((1, LANE)), _const_spec((1, LANE)), _const_spec((1, SSM_D))]
        + [_const_spec(e.shape) for e in expanders],
        out_specs=[chunk(SSM_D, fwd), chunk(SSM_D, bwd)],
        out_shape=[jax.ShapeDtypeStruct((b, s, SSM_D), F32)] * 2,
        scratch_shapes=[pltpu.VMEM(hshape, F32), pltpu.VMEM(hshape, F32)],
        compiler_params=_params(("parallel", "arbitrary")),
        name="ssd",
    )(xbc, xbc, dt, dt, bias, a_row, dskip, *expanders)


def _flash_kernel(q_ref, k_ref, v_ref, o_ref, *, tk):
    s_len = k_ref.shape[1]
    tq = q_ref.shape[1]
    sls = [slice(hh * HEAD_PAD, (hh + 1) * HEAD_PAD) for hh in range(2)]

    def body(j, carry):
        off = pl.multiple_of(j * tk, tk)
        new = []
        for sl, (m, acc) in zip(sls, carry):
            kj = k_ref[0, pl.ds(off, tk), sl]
            vj = v_ref[0, pl.ds(off, tk), sl]
            s = _dot_nt(q_ref[0, :, sl], kj)
            m_new = jnp.maximum(m, jnp.max(s, axis=1, keepdims=True))
            alpha = jnp.exp2(m - m_new)
            p = jnp.exp2(s - m_new).astype(BF16)
            new.append((m_new, acc * alpha + _dot(p, vj)))
        return tuple(new)

    m0 = jnp.full((tq, 1), -jnp.inf, F32)
    acc0 = jnp.zeros((tq, HEAD_PAD), F32)
    carry = lax.fori_loop(0, s_len // tk, body, ((m0, acc0), (m0, acc0)), unroll=ATTN_UNROLL)
    outs = [acc / acc[:, V_ONE_LANE:V_ONE_LANE + 1] for _, acc in carry]
    lane = lax.broadcasted_iota(jnp.int32, (tq, HEAD_PAD), 1)
    o_ref[0] = jnp.where(lane < V_DIM, outs[0], pltpu.roll(outs[1], V_DIM, 1)).astype(BF16)


def _flash(q, k, v):
    b, s, hp = q.shape
    tq = min(ATTN_TQ, s)
    tk = min(ATTN_TK, s)
    pairs = MLA_HEADS // 2
    pw = 2 * HEAD_PAD
    return pl.pallas_call(
        functools.partial(_flash_kernel, tk=tk),
        grid=(b, pairs, s // tq),
        in_specs=[pl.BlockSpec((1, tq, pw), lambda i, p, j: (i, j, p)),
                  pl.BlockSpec((1, s, pw), lambda i, p, j: (i, 0, p)),
                  pl.BlockSpec((1, s, pw), lambda i, p, j: (i, 0, p))],
        out_specs=pl.BlockSpec((1, tq, 2 * V_DIM), lambda i, p, j: (i, j, p)),
        out_shape=jax.ShapeDtypeStruct((b, s, MLA_D), BF16),
        compiler_params=_params(("parallel", "parallel", "arbitrary")),
        name="mla_flash",
    )(q, k, v)


def _cross_attn(h1, kv_ref, pre_w, wq_ref, wo_ref, post_w):
    hn = _rms(h1, pre_w).astype(BF16)
    q = _dot(hn, wq_ref[...]).astype(BF16)
    heads = []
    for hd in range(XA_HEADS):
        sl = slice(hd * XA_HEAD_DIM, (hd + 1) * XA_HEAD_DIM)
        kh = kv_ref[0, :, sl]
        vh = kv_ref[0, :, D_MODEL + hd * XA_HEAD_DIM:D_MODEL + (hd + 1) * XA_HEAD_DIM]
        s = _dot_nt(q[:, sl], kh)
        p = jnp.exp(s - jnp.max(s, axis=1, keepdims=True))
        l = jnp.sum(p, axis=1, keepdims=True)
        heads.append((_dot(p.astype(BF16), vh) / l).astype(BF16))
    o = jnp.concatenate(heads, axis=1)
    xa = _dot(o, wo_ref[...])
    return h1 + _rms(xa, post_w)


def _postmix_even_kernel(h_ref, yf_ref, yb_ref, z_ref, o_ref, snw_ref, wout_ref, mpost_ref,
                         kv_ref, xpre_ref, wq_ref, wo_ref, xpost_ref, out_ref):
    y = (yf_ref[0] + yb_ref[0]) * _silu(z_ref[0])
    gw = SSM_D // SSM_GROUPS
    parts = []
    for g in range(SSM_GROUPS):
        yg = y[:, g * gw:(g + 1) * gw]
        parts.append(yg * lax.rsqrt(jnp.mean(yg * yg, axis=-1, keepdims=True) + EPS))
    y_ssd = (jnp.concatenate(parts, axis=1) * snw_ref[...]).astype(BF16)
    mix = _dot(y_ssd, wout_ref[:SSM_D, :]) + _dot(o_ref[0], wout_ref[SSM_D:, :])
    h1 = h_ref[0] + _rms(mix, mpost_ref[...])
    out_ref[0] = _cross_attn(h1, kv_ref, xpre_ref[...], wq_ref, wo_ref, xpost_ref[...])


def _postmix_odd_kernel(h_ref, f_ref, wmix_ref, mpost_ref,
                        kv_ref, xpre_ref, wq_ref, wo_ref, xpost_ref, out_ref):
    mix = _dot(f_ref[0], wmix_ref[...])
    h1 = h_ref[0] + _rms(mix, mpost_ref[...])
    out_ref[0] = _cross_attn(h1, kv_ref, xpre_ref[...], wq_ref, wo_ref, xpost_ref[...])


def _postmix(kernel, h, mixed, mixed_w, consts_a, kv, kv_off, consts_b):
    b, s, d = h.shape
    t = TOKEN_TILE
    tok = lambda w: pl.BlockSpec((1, t, w), lambda i, j: (i, j, 0))
    in_specs = [tok(d)] + [tok(w) for w in mixed_w]
    in_specs += [_const_spec(c.shape) for c in consts_a]
    in_specs += [pl.BlockSpec((1, N_MEM, 2 * d), lambda i, j: (i + kv_off, 0, 0))]
    in_specs += [_const_spec(c.shape) for c in consts_b]
    return pl.pallas_call(
        kernel,
        grid=(b, s // t),
        in_specs=in_specs,
        out_specs=tok(d),
        out_shape=jax.ShapeDtypeStruct((b, s, d), F32),
        compiler_params=_params(("parallel", "parallel")),
        name="postmix",
    )(h, *mixed, *consts_a, kv, *consts_b)


def _ffn_kernel(h_ref, pre_ref, wg_ref, wu_ref, wd_ref, post_ref, *rest):
    out_ref = rest[-2] if len(rest) == 3 else rest[0]
    h = h_ref[...]
    hn = _rms(h, pre_ref[...]).astype(BF16)
    d_ff = wg_ref.shape[1]
    acc = jnp.zeros(h.shape, F32)
    for c in range(d_ff // FFN_CHUNK):
        sl = slice(c * FFN_CHUNK, (c + 1) * FFN_CHUNK)
        g = _dot(hn, wg_ref[:, sl])
        u = _dot(hn, wu_ref[:, sl])
        acc = acc + _dot((_silu(g) * u).astype(BF16), wd_ref[sl, :])
    out = h + _rms(acc, post_ref[...])
    out_ref[...] = out
    if len(rest) == 3:
        rest[2][...] = _rms(out, rest[0][...]).astype(BF16)


def _ffn(h2, pre, wg, wu, wd, post, next_pre=None):
    n, d = h2.shape
    t = TOKEN_TILE
    tok = pl.BlockSpec((t, d), lambda i: (i, 0))
    in_specs = [tok, _const_spec((1, d)), _const_spec(wg.shape), _const_spec(wu.shape),
                _const_spec(wd.shape), _const_spec((1, d))]
    args = [h2, pre, wg, wu, wd, post]
    out_specs, out_shape = tok, jax.ShapeDtypeStruct((n, d), F32)
    if next_pre is not None:
        in_specs.append(_const_spec((1, d)))
        args.append(next_pre)
        out_specs, out_shape = [tok, tok], [out_shape, jax.ShapeDtypeStruct((n, d), BF16)]
    return pl.pallas_call(
        _ffn_kernel,
        grid=(n // t,),
        in_specs=in_specs,
        out_specs=out_specs,
        out_shape=out_shape,
        compiler_params=_params(("parallel",)),
        name="ffn",
    )(*args)


def _fnet_a_kernel(x_ref, cs_ref, m1_ref, a_ref):
    gd = FOURIER_GROUP_DIM
    n1 = x_ref.shape[1]
    n_col = x_ref.shape[2] // D_MODEL
    slabs = [x_ref[0, :, c * D_MODEL + g * gd:c * D_MODEL + (g + 1) * gd]
             for c in range(n_col) for g in range(FOURIER_GROUPS)]
    y = _dot(jnp.concatenate(slabs, axis=0), cs_ref[...])
    for c in range(n_col):
        rows = [slice((c * FOURIER_GROUPS + g) * n1, (c * FOURIER_GROUPS + g + 1) * n1)
                for g in range(FOURIER_GROUPS)]
        yr = jnp.concatenate([y[r, :gd] for r in rows], axis=1)
        yi = jnp.concatenate([y[r, gd:] for r in rows], axis=1)
        stack = jnp.concatenate([yr, yi], axis=0).astype(BF16)
        a_ref[0, :, c * D_MODEL:(c + 1) * D_MODEL] = _dot(m1_ref[...], stack).astype(BF16)


def _fnet_b_kernel(a_ref, g_ref, f_ref):
    for c in range(g_ref.shape[0]):
        stack = jnp.concatenate([a_ref[0, 0, c], a_ref[0, 1, c]], axis=0)
        f_ref[0, :, c * D_MODEL:(c + 1) * D_MODEL] = _dot(g_ref[c], stack).astype(BF16)


def _fnet(xn, cs, m1, gt):
    b, s, d = xn.shape
    n1 = m1.shape[0] // 2
    n2 = s // n1
    ca = min(FNET_COLS, n2)
    cb = min(FNET_COLS, n1)
    a = pl.pallas_call(
        _fnet_a_kernel,
        grid=(b, n2 // ca),
        in_specs=[pl.BlockSpec((1, n1, ca * d), lambda i, j: (i, 0, j)),
                  _const_spec(cs.shape), _const_spec(m1.shape)],
        out_specs=pl.BlockSpec((1, 2 * n1, ca * d), lambda i, j: (i, 0, j)),
        out_shape=jax.ShapeDtypeStruct((b, 2 * n1, n2 * d), BF16),
        compiler_params=_params(("parallel", "parallel")),
        name="fnet_a",
    )(xn.reshape(b, n1, n2 * d), cs, m1)
    f = pl.pallas_call(
        _fnet_b_kernel,
        grid=(b, n1 // cb),
        in_specs=[pl.BlockSpec((1, 2, cb, n2, d), lambda i, j: (i, 0, j, 0, 0)),
                  pl.BlockSpec((cb, n2, 2 * n2), lambda i, j: (j, 0, 0))],
        out_specs=pl.BlockSpec((1, n2, cb * d), lambda i, j: (i, 0, j)),
        out_shape=jax.ShapeDtypeStruct((b, n2, n1 * d), BF16),
        compiler_params=_params(("parallel", "parallel")),
        name="fnet_b",
    )(a.reshape(b, 2, n1, n2, d), gt)
    return f.reshape(b, s, d)


def _rope_tables(s):
    inv = ROPE_THETA ** (-jnp.arange(0, QK_ROPE, 2, dtype=F32) / QK_ROPE)
    ang = jnp.arange(s, dtype=F32)[:, None] * inv[None, :]
    cos2 = jnp.concatenate([jnp.cos(ang), jnp.cos(ang)], axis=1)
    sin2 = jnp.concatenate([jnp.sin(ang), jnp.sin(ang)], axis=1)
    pad = HEAD_PAD - QK_NOPE - QK_ROPE
    cos_t = jnp.concatenate([jnp.ones((s, QK_NOPE), F32), cos2, jnp.ones((s, pad), F32)], axis=1)
    sin_t = jnp.concatenate([jnp.zeros((s, QK_NOPE), F32), sin2, jnp.zeros((s, pad), F32)], axis=1)
    return cos_t, sin_t


def _rot_cols(w):
    half = w.shape[-1] // 2
    return jnp.concatenate([-w[..., half:], w[..., :half]], axis=-1)


def _pad_cols(w, left, total):
    return jnp.pad(w, ((0, 0), (left, total - left - w.shape[1])))


def _even_weights(w_in, w_uq, w_ukv):
    o1 = SSM_D
    o2 = o1 + CONV_CH
    o3 = o2 + 2 * SSM_HEADS
    o4 = o3 + Q_LORA
    o5 = o4 + KV_LORA
    w_z, w_xbc, w_dt, w_cq, w_ckv, w_kr = (w_in[:, :o1], w_in[:, o1:o2], w_in[:, o2:o3],
                                             w_in[:, o3:o4], w_in[:, o4:o5], w_in[:, o5:])
    win = jnp.concatenate([
        w_z, w_xbc, w_cq, w_ckv,
        _pad_cols(w_kr, QK_NOPE, LANE), _pad_cols(_rot_cols(w_kr), QK_NOPE, LANE),
        _pad_cols(w_dt, 0, LANE)], axis=1).astype(BF16)
    dq = QK_NOPE + QK_ROPE
    wq = w_uq.reshape(Q_LORA, MLA_HEADS, dq)
    zq = jnp.zeros((Q_LORA, MLA_HEADS, HEAD_PAD - dq), F32)
    wqm = jnp.concatenate([wq, zq], axis=-1).reshape(Q_LORA, -1).astype(BF16)
    wqr = jnp.concatenate([jnp.zeros((Q_LORA, MLA_HEADS, QK_NOPE), F32),
                           _rot_cols(wq[..., QK_NOPE:]), zq], axis=-1).reshape(Q_LORA, -1).astype(BF16)
    wkv = w_ukv.reshape(KV_LORA, MLA_HEADS, QK_NOPE + V_DIM)
    zk = jnp.zeros((KV_LORA, MLA_HEADS, HEAD_PAD - QK_NOPE), F32)
    wk = jnp.concatenate([wkv[..., :QK_NOPE], zk], axis=-1).reshape(KV_LORA, -1).astype(BF16)
    zv = jnp.zeros((KV_LORA, MLA_HEADS, HEAD_PAD - V_DIM), F32)
    wv = jnp.concatenate([wkv[..., QK_NOPE:], zv], axis=-1).reshape(KV_LORA, -1).astype(BF16)
    return win, wqm, wqr, wk, wv


def _fnet_tables(s):
    n2 = CHUNK
    n1 = s // n2
    gd = FOURIER_GROUP_DIM
    ci = jnp.arange(gd, dtype=jnp.int32)
    ang_c = (2.0 * math.pi / gd) * ((ci[:, None] * ci[None, :]) % gd).astype(F32)
    cs = (jnp.concatenate([jnp.cos(ang_c), -jnp.sin(ang_c)], axis=1) * gd ** -0.5).astype(BF16)
    i1 = jnp.arange(n1, dtype=jnp.int32)
    ang1 = (2.0 * math.pi / n1) * ((i1[:, None] * i1[None, :]) % n1).astype(F32)
    c1, s1 = jnp.cos(ang1), jnp.sin(ang1)
    m1 = jnp.concatenate([jnp.concatenate([c1, s1], axis=1),
                          jnp.concatenate([-s1, c1], axis=1)], axis=0).astype(BF16)
    i2 = jnp.arange(n2, dtype=jnp.int32)
    ang_a = (2.0 * math.pi / s) * (i1[:, None] * i2[None, :]).astype(F32)
    ang_b = (2.0 * math.pi / n2) * ((i2[:, None] * i2[None, :]) % n2).astype(F32)
    ca, sa = jnp.cos(ang_a)[:, None, :], jnp.sin(ang_a)[:, None, :]
    cb, sb = jnp.cos(ang_b)[None, :, :], jnp.sin(ang_b)[None, :, :]
    gt = (jnp.concatenate([ca * cb - sa * sb, sa * cb + ca * sb], axis=2) * s ** -0.5).astype(BF16)
    return cs, m1, gt


def _row(v, width=None):
    v = v.astype(F32).reshape(1, -1)
    if width is not None:
        v = jnp.pad(v, ((0, 0), (0, width - v.shape[1])))
    return v


def _trunk(x, kv_layers, kv_off, p):
    b, s, d = x.shape
    n = b * s
    h = x
    assert d == D_MODEL and s % min(ATTN_TQ, s) == 0 and s % min(ATTN_TK, s) == 0
    assert s % TOKEN_TILE == 0 and s % (SSD_CHUNKS_PER_STEP * CHUNK) == 0 and (s // CHUNK) % SUBLANE == 0
    z, xbc, dt, q, k, v = _inproj(h.reshape(n, d), p["mix_pre"][0], p["win"], p["q_norm"], p["kv_norm"],
                                  p["wqm"], p["wqr"], p["wk"], p["wv"], p["cos"][:s], p["sin"][:s],
                                  p["conv_w"], p["conv_b"], s)
    yf, yb = _ssd(xbc.reshape(b, s, -1), dt.reshape(b, s, -1), p["dt_bias"], p["a_row"], p["d_skip"])
    hp = MLA_HEADS * HEAD_PAD
    o = _flash(q.reshape(b, s, hp), k.reshape(b, s, hp), v.reshape(b, s, hp))
    h = _postmix(_postmix_even_kernel, h, (yf, yb, z.reshape(b, s, -1), o),
                 (SSM_D, SSM_D, SSM_D, MLA_D),
                 (p["ssm_norm"], p["w_out"], p["mix_post"][0]), kv_layers[0], kv_off,
                 (p["xa_pre"][0], p["xa_wq"][0], p["xa_wo"][0], p["xa_post"][0]))
    h, hn = _ffn(h.reshape(n, d), p["ffn_pre"][0], p["wg"][0], p["wu"][0], p["wd"][0],
                 p["ffn_post"][0], next_pre=p["mix_pre"][1])
    h = h.reshape(b, s, d)
    cs, m1, gt = _fnet_tables(s)
    f = _fnet(hn.reshape(b, s, d), cs, m1, gt)
    h = _postmix(_postmix_odd_kernel, h, (f,), (d,), (p["w_mix"], p["mix_post"][1]),
                 kv_layers[1], kv_off,
                 (p["xa_pre"][1], p["xa_wq"][1], p["xa_wo"][1], p["xa_post"][1]))
    h = _ffn(h.reshape(n, d), p["ffn_pre"][1], p["wg"][1], p["wu"][1], p["wd"][1],
             p["ffn_post"][1]).reshape(b, s, d)
    return h


def kernel(x_prompt, x_sample, mem_prompt, mem_sample, norm_mix_pre, norm_mix_post, norm_xa_pre, norm_xa_post, norm_mem, xa_wq, xa_wkv, xa_wo, norm_ffn_pre, norm_ffn_post, ffn_w_gu, ffn_w_down, ev_w_in, ev_conv_w, ev_conv_b, ev_a_log_f, ev_a_log_b, ev_dt_bias_f, ev_dt_bias_b, ev_d_skip, ev_ssm_norm, ev_q_norm, ev_w_uq, ev_kv_norm, ev_w_ukv, ev_w_out, od_w_mix):
    depth = norm_mix_pre.shape[0]
    assert depth == 2 and ev_w_in.shape[0] == 1 and od_w_mix.shape[0] == 1
    d_ff = ffn_w_down.shape[1]
    assert d_ff % FFN_CHUNK == 0
    s_max = max(x_prompt.shape[1], x_sample.shape[1])
    cos_t, sin_t = _rope_tables(s_max)
    win, wqm, wqr, wk, wv = _even_weights(ev_w_in[0], ev_w_uq[0], ev_w_ukv[0])
    rows = lambda w: [_row(w[i]) for i in range(depth)]
    p = {
        "mix_pre": rows(norm_mix_pre), "mix_post": rows(norm_mix_post),
        "xa_pre": rows(norm_xa_pre), "xa_post": rows(norm_xa_post),
        "ffn_pre": rows(norm_ffn_pre), "ffn_post": rows(norm_ffn_post),
        "xa_wq": [(xa_wq[i] * XA_SCALE).astype(BF16) for i in range(depth)],
        "xa_wo": [xa_wo[i].astype(BF16) for i in range(depth)],
        "wg": [ffn_w_gu[i, :, :d_ff].astype(BF16) for i in range(depth)],
        "wu": [ffn_w_gu[i, :, d_ff:].astype(BF16) for i in range(depth)],
        "wd": [ffn_w_down[i].astype(BF16) for i in range(depth)],
        "win": win, "wqm": wqm, "wqr": wqr, "wk": wk, "wv": wv,
        "q_norm": _row(ev_q_norm[0]), "kv_norm": _row(ev_kv_norm[0]),
        "cos": cos_t, "sin": sin_t,
        "conv_w": jnp.pad(ev_conv_w[0].astype(F32), ((0, SUBLANE - D_CONV), (0, 0))),
        "conv_b": _row(ev_conv_b[0]),
        "dt_bias": _row(jnp.concatenate([ev_dt_bias_f[0], ev_dt_bias_b[0]]), LANE),
        "a_row": _row(-jnp.exp(jnp.concatenate([ev_a_log_f[0], ev_a_log_b[0]]).astype(F32)), LANE),
        "d_skip": _row(jnp.repeat(ev_d_skip[0].astype(F32), SSM_HEAD_DIM)),
        "ssm_norm": _row(ev_ssm_norm[0]),
        "w_out": ev_w_out[0].astype(BF16),
        "w_mix": od_w_mix[0].astype(BF16),
    }
    mem = jnp.concatenate([mem_prompt, mem_sample], axis=0)
    kv_layers = [_memkv(mem, _row(norm_mem[i]), xa_wkv[i].astype(BF16)) for i in range(depth)]
    y_prompt = _trunk(x_prompt, kv_layers, 0, p)
    y_sample = _trunk(x_sample, kv_layers, x_prompt.shape[0], p)
    return (y_prompt, y_sample)
```

```python
import functools
import math

import jax
import jax.numpy as jnp
from jax import lax
from jax.experimental import pallas as pl
from jax.experimental.pallas import tpu as pltpu

F32 = jnp.float32
BF16 = jnp.bfloat16

EPS = 1e-6
D_MODEL = 1024
N_MEM = 256

SSM_HEADS = 8
SSM_HEAD_DIM = 64
SSM_D = SSM_HEADS * SSM_HEAD_DIM
SSM_GROUPS = 2
SSM_HPG = SSM_HEADS // SSM_GROUPS
SSM_STATE = 128
D_CONV = 5
CONV_CH = SSM_D + 2 * SSM_GROUPS * SSM_STATE
CHUNK = 128

MLA_HEADS = 8
QK_NOPE = 64
QK_ROPE = 32
V_DIM = 64
Q_LORA = 256
KV_LORA = 128
ROPE_THETA = 10000.0
MLA_D = MLA_HEADS * V_DIM

FOURIER_GROUPS = 4
FOURIER_GROUP_DIM = D_MODEL // FOURIER_GROUPS

XA_HEADS = 4
XA_HEAD_DIM = D_MODEL // XA_HEADS
XA_SCALE = XA_HEAD_DIM ** -0.5
assert math.frexp(XA_SCALE)[0] == 0.5

LANE = 128
SUBLANE = 8
HEAD_PAD = LANE
V_ONE_LANE = V_DIM
VMEM_LIMIT = 56 * 1024 * 1024

TOKEN_TILE = 512
FFN_CHUNK = 256
ATTN_TQ = 1024
ATTN_TK = 2048
ATTN_UNROLL = 4
FNET_COLS = 16
SSD_CHUNKS_PER_STEP = 4
LOG2E = 1.4426950408889634

O_Z = 0
O_XBC = O_Z + SSM_D
O_CQ = O_XBC + CONV_CH
O_CKV = O_CQ + Q_LORA
O_KA = O_CKV + KV_LORA
O_KB = O_KA + LANE
O_DT = O_KB + LANE
D_IN_PAD = O_DT + LANE


def _params(sem, vmem=VMEM_LIMIT):
    return pltpu.CompilerParams(dimension_semantics=sem, vmem_limit_bytes=vmem)


def _rms(x, w):
    ms = jnp.mean(x * x, axis=-1, keepdims=True)
    return x * lax.rsqrt(ms + EPS) * w


def _silu(x):
    return x / (1.0 + jnp.exp(-x))


def _dot(a, b):
    return jnp.dot(a, b, preferred_element_type=F32)


def _dot_nt(a, b):
    return lax.dot_general(a, b, (((1,), (1,)), ((), ())), preferred_element_type=F32)


def _const_spec(shape):
    nd = len(shape)
    return pl.BlockSpec(shape, lambda *_: (0,) * nd)


def _memkv_kernel(mem_ref, nw_ref, wkv_ref, kv_ref):
    xn = _rms(mem_ref[0], nw_ref[...]).astype(BF16)
    kv_ref[0] = _dot(xn, wkv_ref[...]).astype(BF16)


def _memkv(mem, nw, wkv):
    b, m, d = mem.shape
    return pl.pallas_call(
        _memkv_kernel,
        grid=(b,),
        in_specs=[pl.BlockSpec((1, m, d), lambda i: (i, 0, 0)),
                  _const_spec((1, d)),
                  _const_spec((d, 2 * d))],
        out_specs=pl.BlockSpec((1, m, 2 * d), lambda i: (i, 0, 0)),
        out_shape=jax.ShapeDtypeStruct((b, m, 2 * d), BF16),
        compiler_params=_params(("parallel",)),
        name="memkv",
    )(mem, nw, wkv)


def _inproj_kernel(xp_ref, x_ref, xn_ref, nw_ref, win_ref, qnw_ref, kvnw_ref, wqm_ref, wqr_ref,
                   wk_ref, wv_ref, cos_ref, sin_ref, cw_ref, cb_ref,
                   z_ref, xbc_ref, dt_ref, q_ref, k_ref, v_ref, *, tiles_per_seq):
    t = x_ref.shape[0]
    pos = pl.program_id(0) % tiles_per_seq
    x_ext = jnp.concatenate([xp_ref[...], x_ref[...], xn_ref[...]], axis=0)
    hn = _rms(x_ext, nw_ref[...]).astype(BF16)
    proj_ext = _dot(hn, win_ref[...])
    proj = proj_ext[SUBLANE:SUBLANE + t, :]
    z_ref[...] = proj[:, O_Z:O_XBC]
    dt_ref[...] = proj[:, O_DT:D_IN_PAD]
    row = lax.broadcasted_iota(jnp.int32, (t + 2 * SUBLANE, 1), 0)
    inside = ((row >= SUBLANE) | (pos > 0)) & ((row < SUBLANE + t) | (pos < tiles_per_seq - 1))
    xbc_ext = jnp.where(inside, proj_ext[:, O_XBC:O_CQ], 0.0)
    acc = cb_ref[...] + jnp.zeros((t, CONV_CH), F32)
    n_ext = t + 2 * SUBLANE
    for j in range(D_CONV):
        tap = pltpu.roll(xbc_ext, (D_CONV // 2 - j) % n_ext, 0)[SUBLANE:SUBLANE + t, :]
        acc = acc + tap * cw_ref[j:j + 1, :]
    xbc_ref[...] = _silu(acc).astype(BF16)
    cqn = _rms(proj[:, O_CQ:O_CKV], qnw_ref[...]).astype(BF16)
    ckvn = _rms(proj[:, O_CKV:O_KA], kvnw_ref[...]).astype(BF16)
    cos_t = cos_ref[...]
    sin_t = sin_ref[...]
    qscale = LOG2E * (QK_NOPE + QK_ROPE) ** -0.5
    cos_q = cos_t * qscale
    sin_q = sin_t * qscale
    qm = _dot(cqn, wqm_ref[...])
    qr = _dot(cqn, wqr_ref[...])
    kr = proj[:, O_KA:O_KB] * cos_t + proj[:, O_KB:O_DT] * sin_t
    km = _dot(ckvn, wk_ref[...])
    vm = _dot(ckvn, wv_ref[...])
    lane = lax.broadcasted_iota(jnp.int32, (1, HEAD_PAD), 1)
    one_col = jnp.where(lane == V_ONE_LANE, 1.0, 0.0).astype(F32)
    for h in range(MLA_HEADS):
        sl = slice(h * HEAD_PAD, (h + 1) * HEAD_PAD)
        q_ref[:, sl] = (qm[:, sl] * cos_q + qr[:, sl] * sin_q).astype(BF16)
        k_ref[:, sl] = (km[:, sl] + kr).astype(BF16)
        v_ref[:, sl] = (vm[:, sl] + one_col).astype(BF16)


def _inproj(x2, nw, win, qnw, kvnw, wqm, wqr, wk, wv, cos_t, sin_t, cw, cb, seq):
    n, d = x2.shape
    t = TOKEN_TILE
    tiles_per_seq = seq // t
    rb = t // SUBLANE
    nrb = n // SUBLANE
    tok = lambda w: pl.BlockSpec((t, w), lambda i: (i, 0))
    halo_prev = pl.BlockSpec((SUBLANE, d), lambda i: (jnp.maximum(i * rb - 1, 0), 0))
    halo_next = pl.BlockSpec((SUBLANE, d), lambda i: (jnp.minimum(i * rb + rb, nrb - 1), 0))
    pos = pl.BlockSpec((t, HEAD_PAD), lambda i: (i % tiles_per_seq, 0))
    hp = MLA_HEADS * HEAD_PAD
    outs = [jax.ShapeDtypeStruct((n, SSM_D), F32), jax.ShapeDtypeStruct((n, CONV_CH), BF16),
            jax.ShapeDtypeStruct((n, LANE), F32), jax.ShapeDtypeStruct((n, hp), BF16),
            jax.ShapeDtypeStruct((n, hp), BF16), jax.ShapeDtypeStruct((n, hp), BF16)]
    return pl.pallas_call(
        functools.partial(_inproj_kernel, tiles_per_seq=tiles_per_seq),
        grid=(n // t,),
        in_specs=[halo_prev, tok(d), halo_next, _const_spec((1, d)), _const_spec((d, D_IN_PAD)),
                  _const_spec((1, Q_LORA)), _const_spec((1, KV_LORA)),
                  _const_spec((Q_LORA, hp)), _const_spec((Q_LORA, hp)),
                  _const_spec((KV_LORA, hp)), _const_spec((KV_LORA, hp)),
                  pos, pos, _const_spec((SUBLANE, CONV_CH)), _const_spec((1, CONV_CH))],
        out_specs=[tok(SSM_D), tok(CONV_CH), tok(LANE), tok(hp), tok(hp), tok(hp)],
        out_shape=outs,
        compiler_params=_params(("parallel",)),
        name="inproj",
    )(x2, x2, x2, nw, win, qnw, kvnw, wqm, wqr, wk, wv, cos_t, sin_t, cw, cb)


def _softplus(x):
    return jnp.maximum(x, 0.0) + jnp.log1p(jnp.exp(-jnp.abs(x)))


def _dot_pieces(a, passes, fn):
    out = None
    rem = a
    for _ in range(passes):
        piece = rem.astype(BF16)
        rem = rem - piece.astype(F32)
        term = fn(piece)
        out = term if out is None else out + term
    return out


def _ssd_chunk(xbc, dt_raw, dtb_ref, a_ref, e64_ref, h_in, reverse, lane_off):
    L = CHUNK
    P = SSM_HEAD_DIM
    gw = SSM_HPG * P
    xs = xbc[:, :SSM_D].astype(F32)
    bm = xbc[:, SSM_D:SSM_D + SSM_GROUPS * SSM_STATE]
    cm = xbc[:, SSM_D + SSM_GROUPS * SSM_STATE:]
    dt = _softplus(dt_raw + dtb_ref[...])
    dta = dt * a_ref[...]
    row = lax.broadcasted_iota(jnp.int32, (L, L), 0)
    col = lax.broadcasted_iota(jnp.int32, (L, L), 1)
    mask = (col >= row) if reverse else (col <= row)
    tri = jnp.where(mask, 1.0, 0.0).astype(BF16)
    cum = _dot_pieces(dta, 3, lambda piece: _dot(tri, piece))
    cum_t = cum.T
    dt_t = dt.T
    last = 0 if reverse else L - 1
    exp_cum = jnp.exp(cum)
    w_state = dt * jnp.exp(cum[last:last + 1, :] - cum)
    scal64 = _dot_pieces(jnp.concatenate([w_state, exp_cum], axis=0), 2,
                         lambda piece: _dot(piece, e64_ref[...]))
    xw = (xs * scal64[:L]).astype(BF16)
    exp_cum64 = scal64[L:]
    exp_total64 = exp_cum64[last:last + 1, :]
    lane = lax.broadcasted_iota(jnp.int32, (L, 2 * P), 1)
    ys = []
    h_out = []
    for g in range(SSM_GROUPS):
        bg = bm[:, g * SSM_STATE:(g + 1) * SSM_STATE]
        cg = cm[:, g * SSM_STATE:(g + 1) * SSM_STATE]
        cb = _dot_nt(cg, bg)
        h_prev = h_in[g]
        y_off = _dot(cg, h_prev.astype(BF16)) * exp_cum64[:, g * gw:(g + 1) * gw]
        pairs = []
        for pr in range(SSM_HPG // 2):
            x_pair = xbc[:, g * gw + pr * 2 * P:g * gw + (pr + 1) * 2 * P]
            ms = []
            for q in range(2):
                hh = g * SSM_HPG + pr * 2 + q
                ln = lane_off + hh
                seg = cum[:, ln:ln + 1] - cum_t[ln:ln + 1, :]
                dec = jnp.exp(jnp.where(mask, seg, -jnp.inf))
                ms.append((cb * dec * dt_t[ln:ln + 1, :]).astype(BF16))
            y2 = _dot(jnp.concatenate(ms, axis=0), x_pair)
            pairs.append(jnp.where(lane < P, y2[:L], y2[L:]))
        ys.append(y_off + jnp.concatenate(pairs, axis=1))
        st = _dot(bg.astype(F32).T.astype(BF16), xw[:, g * gw:(g + 1) * gw])
        h_out.append(h_prev * exp_total64[:, g * gw:(g + 1) * gw] + st)
    return jnp.concatenate(ys, axis=1), xs, h_out


def _ssd_kernel(xf_ref, xb_ref, dtf_ref, dtb_ref, bias_ref, a_ref, dskip_ref,
                e64f_ref, e64b_ref, yf_ref, yb_ref, hf_ref, hb_ref):
    @pl.when(pl.program_id(1) == 0)
    def _():
        hf_ref[...] = jnp.zeros_like(hf_ref)
        hb_ref[...] = jnp.zeros_like(hb_ref)

    n_sub = xf_ref.shape[1] // CHUNK
    h_f = [hf_ref[g] for g in range(SSM_GROUPS)]
    h_b = [hb_ref[g] for g in range(SSM_GROUPS)]
    for i in range(n_sub):
        rows = slice(i * CHUNK, (i + 1) * CHUNK)
        y_f, xs_f, h_f = _ssd_chunk(xf_ref[0, rows, :], dtf_ref[0, rows, :], bias_ref, a_ref,
                                    e64f_ref, h_f, False, 0)
        yf_ref[0, rows, :] = y_f + xs_f * dskip_ref[...]
        rows = slice((n_sub - 1 - i) * CHUNK, (n_sub - i) * CHUNK)
        y_b, _, h_b = _ssd_chunk(xb_ref[0, rows, :], dtb_ref[0, rows, :], bias_ref, a_ref,
                                 e64b_ref, h_b, True, SSM_HEADS)
        yb_ref[0, rows, :] = y_b
    for g in range(SSM_GROUPS):
        hf_ref[g] = h_f[g]
        hb_ref[g] = h_b[g]


def _head_lane_expander(lane_off, width):
    src_lane = lax.broadcasted_iota(jnp.int32, (LANE, SSM_HEADS * width), 0)
    dst_head = lax.broadcasted_iota(jnp.int32, (LANE, SSM_HEADS * width), 1) // width
    return (src_lane == dst_head + lane_off).astype(BF16)


def _ssd(xbc, dt, bias, a_row, dskip):
    b, s, _ = xbc.shape
    expanders = [_head_lane_expander(off, SSM_HEAD_DIM) for off in (0, SSM_HEADS)]
    rows = SSD_CHUNKS_PER_STEP * CHUNK
    nc = s // rows
    chunk = lambda w, f: pl.BlockSpec((1, rows, w), lambda i, c: (i, f(c), 0))
    fwd = lambda c: c
    bwd = lambda c: nc - 1 - c
    hshape = (SSM_GROUPS, SSM_STATE, SSM_HPG * SSM_HEAD_DIM)
    return pl.pallas_call(
        _ssd_kernel,
        grid=(b, nc),
        in_specs=[chunk(CONV_CH, fwd), chunk(CONV_CH, bwd), chunk(LANE, fwd), chunk(LANE, bwd),
                  _const_spec((1, LANE)), _const_spec((1, LANE)), _const_spec((1, SSM_D))]
        + [_const_spec(e.shape) for e in expanders],
        out_specs=[chunk(SSM_D, fwd), chunk(SSM_D, bwd)],
        out_shape=[jax.ShapeDtypeStruct((b, s, SSM_D), F32)] * 2,
        scratch_shapes=[pltpu.VMEM(hshape, F32), pltpu.VMEM(hshape, F32)],
        compiler_params=_params(("parallel", "arbitrary")),
        name="ssd",
    )(xbc, xbc, dt, dt, bias, a_row, dskip, *expanders)


def _flash_kernel(q_ref, k_ref, v_ref, o_ref, *, tk):
    s_len = k_ref.shape[1]
    tq = q_ref.shape[1]
    sls = [slice(hh * HEAD_PAD, (hh + 1) * HEAD_PAD) for hh in range(2)]

    def body(j, carry):
        off = pl.multiple_of(j * tk, tk)
        new = []
        for sl, (m, acc) in zip(sls, carry):
            kj = k_ref[0, pl.ds(off, tk), sl]
            vj = v_ref[0, pl.ds(off, tk), sl]
            s = _dot_nt(q_ref[0, :, sl], kj)
            m_new = jnp.maximum(m, jnp.max(s, axis=1, keepdims=True))
            alpha = jnp.exp2(m - m_new)
            p = jnp.exp2(s - m_new).astype(BF16)
            new.append((m_new, acc * alpha + _dot(p, vj)))
        return tuple(new)

    m0 = jnp.full((tq, 1), -jnp.inf, F32)
    acc0 = jnp.zeros((tq, HEAD_PAD), F32)
    carry = lax.fori_loop(0, s_len // tk, body, ((m0, acc0), (m0, acc0)), unroll=ATTN_UNROLL)
    outs = [acc / acc[:, V_ONE_LANE:V_ONE_LANE + 1] for _, acc in carry]
    lane = lax.broadcasted_iota(jnp.int32, (tq, HEAD_PAD), 1)
    o_ref[0] = jnp.where(lane < V_DIM, outs[0], pltpu.roll(outs[1], V_DIM, 1)).astype(BF16)


def _flash(q, k, v):
    b, s, hp = q.shape
    tq = min(ATTN_TQ, s)
    tk = min(ATTN_TK, s)
    pairs = MLA_HEADS // 2
    pw = 2 * HEAD_PAD
    return pl.pallas_call(
        functools.partial(_flash_kernel, tk=tk),
        grid=(b, pairs, s // tq),
        in_specs=[pl.BlockSpec((1, tq, pw), lambda i, p, j: (i, j, p)),
                  pl.BlockSpec((1, s, pw), lambda i, p, j: (i, 0, p)),
                  pl.BlockSpec((1, s, pw), lambda i, p, j: (i, 0, p))],
        out_specs=pl.BlockSpec((1, tq, 2 * V_DIM), lambda i, p, j: (i, j, p)),
        out_shape=jax.ShapeDtypeStruct((b, s, MLA_D), BF16),
        compiler_params=_params(("parallel", "parallel", "arbitrary")),
        name="mla_flash",
    )(q, k, v)


def _cross_attn(h1, kv_ref, pre_w, wq_ref, wo_ref, post_w):
    hn = _rms(h1, pre_w).astype(BF16)
    q = _dot(hn, wq_ref[...]).astype(BF16)
    heads = []
    for hd in range(XA_HEADS):
        sl = slice(hd * XA_HEAD_DIM, (hd + 1) * XA_HEAD_DIM)
        kh = kv_ref[0, :, sl]
        vh = kv_ref[0, :, D_MODEL + hd * XA_HEAD_DIM:D_MODEL + (hd + 1) * XA_HEAD_DIM]
        s = _dot_nt(q[:, sl], kh)
        p = jnp.exp(s - jnp.max(s, axis=1, keepdims=True))
        l = jnp.sum(p, axis=1, keepdims=True)
        heads.append((_dot(p.astype(BF16), vh) / l).astype(BF16))
    o = jnp.concatenate(heads, axis=1)
    xa = _dot(o, wo_ref[...])
    return h1 + _rms(xa, post_w)


def _postmix_even_kernel(h_ref, yf_ref, yb_ref, z_ref, o_ref, snw_ref, wout_ref, mpost_ref,
                         kv_ref, xpre_ref, wq_ref, wo_ref, xpost_ref, out_ref):
    y = (yf_ref[0] + yb_ref[0]) * _silu(z_ref[0])
    gw = SSM_D // SSM_GROUPS
    parts = []
    for g in range(SSM_GROUPS):
        yg = y[:, g * gw:(g + 1) * gw]
        parts.append(yg * lax.rsqrt(jnp.mean(yg * yg, axis=-1, keepdims=True) + EPS))
    y_ssd = (jnp.concatenate(parts, axis=1) * snw_ref[...]).astype(BF16)
    mix = _dot(y_ssd, wout_ref[:SSM_D, :]) + _dot(o_ref[0], wout_ref[SSM_D:, :])
    h1 = h_ref[0] + _rms(mix, mpost_ref[...])
    out_ref[0] = _cross_attn(h1, kv_ref, xpre_ref[...], wq_ref, wo_ref, xpost_ref[...])


def _postmix_odd_kernel(h_ref, f_ref, wmix_ref, mpost_ref,
                        kv_ref, xpre_ref, wq_ref, wo_ref, xpost_ref, out_ref):
    mix = _dot(f_ref[0], wmix_ref[...])
    h1 = h_ref[0] + _rms(mix, mpost_ref[...])
    out_ref[0] = _cross_attn(h1, kv_ref, xpre_ref[...], wq_ref, wo_ref, xpost_ref[...])


def _postmix(kernel, h, mixed, mixed_w, consts_a, kv, kv_off, consts_b):
    b, s, d = h.shape
    t = TOKEN_TILE
    tok = lambda w: pl.BlockSpec((1, t, w), lambda i, j: (i, j, 0))
    in_specs = [tok(d)] + [tok(w) for w in mixed_w]
    in_specs += [_const_spec(c.shape) for c in consts_a]
    in_specs += [pl.BlockSpec((1, N_MEM, 2 * d), lambda i, j: (i + kv_off, 0, 0))]
    in_specs += [_const_spec(c.shape) for c in consts_b]
    return pl.pallas_call(
        kernel,
        grid=(b, s // t),
        in_specs=in_specs,
        out_specs=tok(d),
        out_shape=jax.ShapeDtypeStruct((b, s, d), F32),
        compiler_params=_params(("parallel", "parallel")),
        name="postmix",
    )(h, *mixed, *consts_a, kv, *consts_b)


def _ffn_kernel(h_ref, pre_ref, wg_ref, wu_ref, wd_ref, post_ref, *rest):
    out_ref = rest[-2] if len(rest) == 3 else rest[0]
    h = h_ref[...]
    hn = _rms(h, pre_ref[...]).astype(BF16)
    d_ff = wg_ref.shape[1]
    acc = jnp.zeros(h.shape, F32)
    for c in range(d_ff // FFN_CHUNK):
        sl = slice(c * FFN_CHUNK, (c + 1) * FFN_CHUNK)
        g = _dot(hn, wg_ref[:, sl])
        u = _dot(hn, wu_ref[:, sl])
        acc = acc + _dot((_silu(g) * u).astype(BF16), wd_ref[sl, :])
    out = h + _rms(acc, post_ref[...])
    out_ref[...] = out
    if len(rest) == 3:
        rest[2][...] = _rms(out, rest[0][...]).astype(BF16)


def _ffn(h2, pre, wg, wu, wd, post, next_pre=None):
    n, d = h2.shape
    t = TOKEN_TILE
    tok = pl.BlockSpec((t, d), lambda i: (i, 0))
    in_specs = [tok, _const_spec((1, d)), _const_spec(wg.shape), _const_spec(wu.shape),
                _const_spec(wd.shape), _const_spec((1, d))]
    args = [h2, pre, wg, wu, wd, post]
    out_specs, out_shape = tok, jax.ShapeDtypeStruct((n, d), F32)
    if next_pre is not None:
        in_specs.append(_const_spec((1, d)))
        args.append(next_pre)
        out_specs, out_shape = [tok, tok], [out_shape, jax.ShapeDtypeStruct((n, d), BF16)]
    return pl.pallas_call(
        _ffn_kernel,
        grid=(n // t,),
        in_specs=in_specs,
        out_specs=out_specs,
        out_shape=out_shape,
        compiler_params=_params(("parallel",)),
        name="ffn",
    )(*args)


def _fnet_a_kernel(x_ref, cs_ref, m1_ref, a_ref):
    gd = FOURIER_GROUP_DIM
    n1 = x_ref.shape[1]
    n_col = x_ref.shape[2]
    xt = pltpu.einshape("kcd->ckd", x_ref[0])
    slabs = [xt[c, :, g * gd:(g + 1) * gd] for c in range(n_col) for g in range(FOURIER_GROUPS)]
    y = _dot(jnp.concatenate(slabs, axis=0), cs_ref[...])
    outs = []
    for c in range(n_col):
        rows = [slice((c * FOURIER_GROUPS + g) * n1, (c * FOURIER_GROUPS + g + 1) * n1)
                for g in range(FOURIER_GROUPS)]
        yr = jnp.concatenate([y[r, :gd] for r in rows], axis=1)
        yi = jnp.concatenate([y[r, gd:] for r in rows], axis=1)
        stack = jnp.concatenate([yr, yi], axis=0).astype(BF16)
        outs.append(_dot(m1_ref[...], stack))
    a = pltpu.einshape("ckd->kcd", jnp.stack(outs, axis=0))
    a_ref[0] = a.reshape(2, n1, n_col, D_MODEL).astype(BF16)


def _fnet_b_kernel(a_ref, g_ref, f_ref):
    outs = []
    for c in range(g_ref.shape[0]):
        stack = jnp.concatenate([a_ref[0, 0, c], a_ref[0, 1, c]], axis=0)
        outs.append(_dot(g_ref[c], stack))
    f_ref[0] = pltpu.einshape("ckd->kcd", jnp.stack(outs, axis=0)).astype(BF16)


def _fnet(xn, cs, m1, gt):
    b, s, d = xn.shape
    n1 = m1.shape[0] // 2
    n2 = s // n1
    ca = min(FNET_COLS, n2)
    cb = min(FNET_COLS, n1)
    a = pl.pallas_call(
        _fnet_a_kernel,
        grid=(b, n2 // ca),
        in_specs=[pl.BlockSpec((1, n1, ca, d), lambda i, j: (i, 0, j, 0)),
                  _const_spec(cs.shape), _const_spec(m1.shape)],
        out_specs=pl.BlockSpec((1, 2, n1, ca, d), lambda i, j: (i, 0, 0, j, 0)),
        out_shape=jax.ShapeDtypeStruct((b, 2, n1, n2, d), BF16),
        compiler_params=_params(("parallel", "parallel")),
        name="fnet_a",
    )(xn.reshape(b, n1, n2, d), cs, m1)
    f = pl.pallas_call(
        _fnet_b_kernel,
        grid=(b, n1 // cb),
        in_specs=[pl.BlockSpec((1, 2, cb, n2, d), lambda i, j: (i, 0, j, 0, 0)),
                  pl.BlockSpec((cb, n2, 2 * n2), lambda i, j: (j, 0, 0))],
        out_specs=pl.BlockSpec((1, n2, cb, d), lambda i, j: (i, 0, j, 0)),
        out_shape=jax.ShapeDtypeStruct((b, n2, n1, d), BF16),
        compiler_params=_params(("parallel", "parallel")),
        name="fnet_b",
    )(a, gt)
    return f.reshape(b, s, d)


def _rope_tables(s):
    inv = ROPE_THETA ** (-jnp.arange(0, QK_ROPE, 2, dtype=F32) / QK_ROPE)
    ang = jnp.arange(s, dtype=F32)[:, None] * inv[None, :]
    cos2 = jnp.concatenate([jnp.cos(ang), jnp.cos(ang)], axis=1)
    sin2 = jnp.concatenate([jnp.sin(ang), jnp.sin(ang)], axis=1)
    pad = HEAD_PAD - QK_NOPE - QK_ROPE
    cos_t = jnp.concatenate([jnp.ones((s, QK_NOPE), F32), cos2, jnp.ones((s, pad), F32)], axis=1)
    sin_t = jnp.concatenate([jnp.zeros((s, QK_NOPE), F32), sin2, jnp.zeros((s, pad), F32)], axis=1)
    return cos_t, sin_t


def _rot_cols(w):
    half = w.shape[-1] // 2
    return jnp.concatenate([-w[..., half:], w[..., :half]], axis=-1)


def _pad_cols(w, left, total):
    return jnp.pad(w, ((0, 0), (left, total - left - w.shape[1])))


def _even_weights(w_in, w_uq, w_ukv):
    o1 = SSM_D
    o2 = o1 + CONV_CH
    o3 = o2 + 2 * SSM_HEADS
    o4 = o3 + Q_LORA
    o5 = o4 + KV_LORA
    w_z, w_xbc, w_dt, w_cq, w_ckv, w_kr = (w_in[:, :o1], w_in[:, o1:o2], w_in[:, o2:o3],
                                             w_in[:, o3:o4], w_in[:, o4:o5], w_in[:, o5:])
    win = jnp.concatenate([
        w_z, w_xbc, w_cq, w_ckv,
        _pad_cols(w_kr, QK_NOPE, LANE), _pad_cols(_rot_cols(w_kr), QK_NOPE, LANE),
        _pad_cols(w_dt, 0, LANE)], axis=1).astype(BF16)
    dq = QK_NOPE + QK_ROPE
    wq = w_uq.reshape(Q_LORA, MLA_HEADS, dq)
    zq = jnp.zeros((Q_LORA, MLA_HEADS, HEAD_PAD - dq), F32)
    wqm = jnp.concatenate([wq, zq], axis=-1).reshape(Q_LORA, -1).astype(BF16)
    wqr = jnp.concatenate([jnp.zeros((Q_LORA, MLA_HEADS, QK_NOPE), F32),
                           _rot_cols(wq[..., QK_NOPE:]), zq], axis=-1).reshape(Q_LORA, -1).astype(BF16)
    wkv = w_ukv.reshape(KV_LORA, MLA_HEADS, QK_NOPE + V_DIM)
    zk = jnp.zeros((KV_LORA, MLA_HEADS, HEAD_PAD - QK_NOPE), F32)
    wk = jnp.concatenate([wkv[..., :QK_NOPE], zk], axis=-1).reshape(KV_LORA, -1).astype(BF16)
    zv = jnp.zeros((KV_LORA, MLA_HEADS, HEAD_PAD - V_DIM), F32)
    wv = jnp.concatenate([wkv[..., QK_NOPE:], zv], axis=-1).reshape(KV_LORA, -1).astype(BF16)
    return win, wqm, wqr, wk, wv


def _fnet_tables(s):
    n2 = CHUNK
    n1 = s // n2
    gd = FOURIER_GROUP_DIM
    ci = jnp.arange(gd, dtype=jnp.int32)
    ang_c = (2.0 * math.pi / gd) * ((ci[:, None] * ci[None, :]) % gd).astype(F32)
    cs = (jnp.concatenate([jnp.cos(ang_c), -jnp.sin(ang_c)], axis=1) * gd ** -0.5).astype(BF16)
    i1 = jnp.arange(n1, dtype=jnp.int32)
    ang1 = (2.0 * math.pi / n1) * ((i1[:, None] * i1[None, :]) % n1).astype(F32)
    c1, s1 = jnp.cos(ang1), jnp.sin(ang1)
    m1 = jnp.concatenate([jnp.concatenate([c1, s1], axis=1),
                          jnp.concatenate([-s1, c1], axis=1)], axis=0).astype(BF16)
    i2 = jnp.arange(n2, dtype=jnp.int32)
    ang_a = (2.0 * math.pi / s) * (i1[:, None] * i2[None, :]).astype(F32)
    ang_b = (2.0 * math.pi / n2) * ((i2[:, None] * i2[None, :]) % n2).astype(F32)
    ca, sa = jnp.cos(ang_a)[:, None, :], jnp.sin(ang_a)[:, None, :]
    cb, sb = jnp.cos(ang_b)[None, :, :], jnp.sin(ang_b)[None, :, :]
    gt = (jnp.concatenate([ca * cb - sa * sb, sa * cb + ca * sb], axis=2) * s ** -0.5).astype(BF16)
    return cs, m1, gt


def _row(v, width=None):
    v = v.astype(F32).reshape(1, -1)
    if width is not None:
        v = jnp.pad(v, ((0, 0), (0, width - v.shape[1])))
    return v


def _trunk(x, kv_layers, kv_off, p):
    b, s, d = x.shape
    n = b * s
    h = x
    assert d == D_MODEL and s % min(ATTN_TQ, s) == 0 and s % min(ATTN_TK, s) == 0
    assert s % TOKEN_TILE == 0 and s % (SSD_CHUNKS_PER_STEP * CHUNK) == 0 and (s // CHUNK) % SUBLANE == 0
    z, xbc, dt, q, k, v = _inproj(h.reshape(n, d), p["mix_pre"][0], p["win"], p["q_norm"], p["kv_norm"],
                                  p["wqm"], p["wqr"], p["wk"], p["wv"], p["cos"][:s], p["sin"][:s],
                                  p["conv_w"], p["conv_b"], s)
    yf, yb = _ssd(xbc.reshape(b, s, -1), dt.reshape(b, s, -1), p["dt_bias"], p["a_row"], p["d_skip"])
    hp = MLA_HEADS * HEAD_PAD
    o = _flash(q.reshape(b, s, hp), k.reshape(b, s, hp), v.reshape(b, s, hp))
    h = _postmix(_postmix_even_kernel, h, (yf, yb, z.reshape(b, s, -1), o),
                 (SSM_D, SSM_D, SSM_D, MLA_D),
                 (p["ssm_norm"], p["w_out"], p["mix_post"][0]), kv_layers[0], kv_off,
                 (p["xa_pre"][0], p["xa_wq"][0], p["xa_wo"][0], p["xa_post"][0]))
    h, hn = _ffn(h.reshape(n, d), p["ffn_pre"][0], p["wg"][0], p["wu"][0], p["wd"][0],
                 p["ffn_post"][0], next_pre=p["mix_pre"][1])
    h = h.reshape(b, s, d)
    cs, m1, gt = _fnet_tables(s)
    f = _fnet(hn.reshape(b, s, d), cs, m1, gt)
    h = _postmix(_postmix_odd_kernel, h, (f,), (d,), (p["w_mix"], p["mix_post"][1]),
                 kv_layers[1], kv_off,
                 (p["xa_pre"][1], p["xa_wq"][1], p["xa_wo"][1], p["xa_post"][1]))
    h = _ffn(h.reshape(n, d), p["ffn_pre"][1], p["wg"][1], p["wu"][1], p["wd"][1],
             p["ffn_post"][1]).reshape(b, s, d)
    return h


def kernel(x_prompt, x_sample, mem_prompt, mem_sample, norm_mix_pre, norm_mix_post, norm_xa_pre, norm_xa_post, norm_mem, xa_wq, xa_wkv, xa_wo, norm_ffn_pre, norm_ffn_post, ffn_w_gu, ffn_w_down, ev_w_in, ev_conv_w, ev_conv_b, ev_a_log_f, ev_a_log_b, ev_dt_bias_f, ev_dt_bias_b, ev_d_skip, ev_ssm_norm, ev_q_norm, ev_w_uq, ev_kv_norm, ev_w_ukv, ev_w_out, od_w_mix):
    depth = norm_mix_pre.shape[0]
    assert depth == 2 and ev_w_in.shape[0] == 1 and od_w_mix.shape[0] == 1
    d_ff = ffn_w_down.shape[1]
    assert d_ff % FFN_CHUNK == 0
    s_max = max(x_prompt.shape[1], x_sample.shape[1])
    cos_t, sin_t = _rope_tables(s_max)
    win, wqm, wqr, wk, wv = _even_weights(ev_w_in[0], ev_w_uq[0], ev_w_ukv[0])
    rows = lambda w: [_row(w[i]) for i in range(depth)]
    p = {
        "mix_pre": rows(norm_mix_pre), "mix_post": rows(norm_mix_post),
        "xa_pre": rows(norm_xa_pre), "xa_post": rows(norm_xa_post),
        "ffn_pre": rows(norm_ffn_pre), "ffn_post": rows(norm_ffn_post),
        "xa_wq": [(xa_wq[i] * XA_SCALE).astype(BF16) for i in range(depth)],
        "xa_wo": [xa_wo[i].astype(BF16) for i in range(depth)],
        "wg": [ffn_w_gu[i, :, :d_ff].astype(BF16) for i in range(depth)],
        "wu": [ffn_w_gu[i, :, d_ff:].astype(BF16) for i in range(depth)],
        "wd": [ffn_w_down[i].astype(BF16) for i in range(depth)],
        "win": win, "wqm": wqm, "wqr": wqr, "wk": wk, "wv": wv,
        "q_norm": _row(ev_q_norm[0]), "kv_norm": _row(ev_kv_norm[0]),
        "cos": cos_t, "sin": sin_t,
        "conv_w": jnp.pad(ev_conv_w[0].astype(F32), ((0, SUBLANE - D_CONV), (0, 0))),
        "conv_b": _row(ev_conv_b[0]),
        "dt_bias": _row(jnp.concatenate([ev_dt_bias_f[0], ev_dt_bias_b[0]]), LANE),
        "a_row": _row(-jnp.exp(jnp.concatenate([ev_a_log_f[0], ev_a_log_b[0]]).astype(F32)), LANE),
        "d_skip": _row(jnp.repeat(ev_d_skip[0].astype(F32), SSM_HEAD_DIM)),
        "ssm_norm": _row(ev_ssm_norm[0]),
        "w_out": ev_w_out[0].astype(BF16),
        "w_mix": od_w_mix[0].astype(BF16),
    }
    mem = jnp.concatenate([mem_prompt, mem_sample], axis=0)
    kv_layers = [_memkv(mem, _row(norm_mem[i]), xa_wkv[i].astype(BF16)) for i in range(depth)]
    y_prompt = _trunk(x_prompt, kv_layers, 0, p)
    y_sample = _trunk(x_sample, kv_layers, x_prompt.shape[0], p)
    return (y_prompt, y_sample)
```

```python
import functools
import math

import jax
import jax.numpy as jnp
from jax import lax
from jax.experimental import pallas as pl
from jax.experimental.pallas import tpu as pltpu

F32 = jnp.float32
BF16 = jnp.bfloat16

EPS = 1e-6
D_MODEL = 1024
N_MEM = 256

SSM_HEADS = 8
SSM_HEAD_DIM = 64
SSM_D = SSM_HEADS * SSM_HEAD_DIM
SSM_GROUPS = 2
SSM_HPG = SSM_HEADS // SSM_GROUPS
SSM_STATE = 128
D_CONV = 5
CONV_CH = SSM_D + 2 * SSM_GROUPS * SSM_STATE
CHUNK = 128

MLA_HEADS = 8
QK_NOPE = 64
QK_ROPE = 32
V_DIM = 64
Q_LORA = 256
KV_LORA = 128
ROPE_THETA = 10000.0
MLA_D = MLA_HEADS * V_DIM

FOURIER_GROUPS = 4
FOURIER_GROUP_DIM = D_MODEL // FOURIER_GROUPS

XA_HEADS = 4
XA_HEAD_DIM = D_MODEL // XA_HEADS
XA_SCALE = XA_HEAD_DIM ** -0.5
assert math.frexp(XA_SCALE)[0] == 0.5

LANE = 128
SUBLANE = 8
HEAD_PAD = LANE
V_ONE_LANE = V_DIM
VMEM_LIMIT = 56 * 1024 * 1024

TOKEN_TILE = 512
FFN_CHUNK = 256
ATTN_TQ = 1024
ATTN_TK = 2048
ATTN_UNROLL = 4
FNET_COLS = 16
SSD_CHUNKS_PER_STEP = 8
LOG2E = 1.4426950408889634

O_Z = 0
O_XBC = O_Z + SSM_D
O_CQ = O_XBC + CONV_CH
O_CKV = O_CQ + Q_LORA
O_KA = O_CKV + KV_LORA
O_KB = O_KA + LANE
O_DT = O_KB + LANE
D_IN_PAD = O_DT + LANE


def _params(sem, vmem=VMEM_LIMIT):
    return pltpu.CompilerParams(dimension_semantics=sem, vmem_limit_bytes=vmem)


def _rms(x, w):
    ms = jnp.mean(x * x, axis=-1, keepdims=True)
    return x * lax.rsqrt(ms + EPS) * w


def _silu(x):
    return x / (1.0 + jnp.exp(-x))


def _dot(a, b):
    return jnp.dot(a, b, preferred_element_type=F32)


def _dot_nt(a, b):
    return lax.dot_general(a, b, (((1,), (1,)), ((), ())), preferred_element_type=F32)


def _const_spec(shape):
    nd = len(shape)
    return pl.BlockSpec(shape, lambda *_: (0,) * nd)


def _memkv_kernel(mem_ref, nw_ref, wkv_ref, kv_ref):
    xn = _rms(mem_ref[0], nw_ref[...]).astype(BF16)
    kv_ref[0] = _dot(xn, wkv_ref[...]).astype(BF16)


def _memkv(mem, nw, wkv):
    b, m, d = mem.shape
    return pl.pallas_call(
        _memkv_kernel,
        grid=(b,),
        in_specs=[pl.BlockSpec((1, m, d), lambda i: (i, 0, 0)),
                  _const_spec((1, d)),
                  _const_spec((d, 2 * d))],
        out_specs=pl.BlockSpec((1, m, 2 * d), lambda i: (i, 0, 0)),
        out_shape=jax.ShapeDtypeStruct((b, m, 2 * d), BF16),
        compiler_params=_params(("parallel",)),
        name="memkv",
    )(mem, nw, wkv)


def _inproj_kernel(xp_ref, x_ref, xn_ref, nw_ref, win_ref, qnw_ref, kvnw_ref, wqm_ref, wqr_ref,
                   wk_ref, wv_ref, cos_ref, sin_ref, cw_ref, cb_ref,
                   z_ref, xbc_ref, dt_ref, q_ref, k_ref, v_ref, *, tiles_per_seq):
    t = x_ref.shape[0]
    pos = pl.program_id(0) % tiles_per_seq
    x_ext = jnp.concatenate([xp_ref[...], x_ref[...], xn_ref[...]], axis=0)
    hn = _rms(x_ext, nw_ref[...]).astype(BF16)
    proj_ext = _dot(hn, win_ref[...])
    proj = proj_ext[SUBLANE:SUBLANE + t, :]
    z_ref[...] = proj[:, O_Z:O_XBC]
    dt_ref[...] = proj[:, O_DT:D_IN_PAD]
    row = lax.broadcasted_iota(jnp.int32, (t + 2 * SUBLANE, 1), 0)
    inside = ((row >= SUBLANE) | (pos > 0)) & ((row < SUBLANE + t) | (pos < tiles_per_seq - 1))
    xbc_ext = jnp.where(inside, proj_ext[:, O_XBC:O_CQ], 0.0)
    acc = cb_ref[...] + jnp.zeros((t, CONV_CH), F32)
    n_ext = t + 2 * SUBLANE
    for j in range(D_CONV):
        tap = pltpu.roll(xbc_ext, (D_CONV // 2 - j) % n_ext, 0)[SUBLANE:SUBLANE + t, :]
        acc = acc + tap * cw_ref[j:j + 1, :]
    xbc_ref[...] = _silu(acc).astype(BF16)
    cqn = _rms(proj[:, O_CQ:O_CKV], qnw_ref[...]).astype(BF16)
    ckvn = _rms(proj[:, O_CKV:O_KA], kvnw_ref[...]).astype(BF16)
    cos_t = cos_ref[...]
    sin_t = sin_ref[...]
    qscale = LOG2E * (QK_NOPE + QK_ROPE) ** -0.5
    cos_q = cos_t * qscale
    sin_q = sin_t * qscale
    qm = _dot(cqn, wqm_ref[...])
    qr = _dot(cqn, wqr_ref[...])
    kr = proj[:, O_KA:O_KB] * cos_t + proj[:, O_KB:O_DT] * sin_t
    km = _dot(ckvn, wk_ref[...])
    vm = _dot(ckvn, wv_ref[...])
    lane = lax.broadcasted_iota(jnp.int32, (1, HEAD_PAD), 1)
    one_col = jnp.where(lane == V_ONE_LANE, 1.0, 0.0).astype(F32)
    for h in range(MLA_HEADS):
        sl = slice(h * HEAD_PAD, (h + 1) * HEAD_PAD)
        q_ref[:, sl] = (qm[:, sl] * cos_q + qr[:, sl] * sin_q).astype(BF16)
        k_ref[:, sl] = (km[:, sl] + kr).astype(BF16)
        v_ref[:, sl] = (vm[:, sl] + one_col).astype(BF16)


def _inproj(x2, nw, win, qnw, kvnw, wqm, wqr, wk, wv, cos_t, sin_t, cw, cb, seq):
    n, d = x2.shape
    t = TOKEN_TILE
    tiles_per_seq = seq // t
    rb = t // SUBLANE
    nrb = n // SUBLANE
    tok = lambda w: pl.BlockSpec((t, w), lambda i: (i, 0))
    halo_prev = pl.BlockSpec((SUBLANE, d), lambda i: (jnp.maximum(i * rb - 1, 0), 0))
    halo_next = pl.BlockSpec((SUBLANE, d), lambda i: (jnp.minimum(i * rb + rb, nrb - 1), 0))
    pos = pl.BlockSpec((t, HEAD_PAD), lambda i: (i % tiles_per_seq, 0))
    hp = MLA_HEADS * HEAD_PAD
    outs = [jax.ShapeDtypeStruct((n, SSM_D), F32), jax.ShapeDtypeStruct((n, CONV_CH), BF16),
            jax.ShapeDtypeStruct((n, LANE), F32), jax.ShapeDtypeStruct((n, hp), BF16),
            jax.ShapeDtypeStruct((n, hp), BF16), jax.ShapeDtypeStruct((n, hp), BF16)]
    return pl.pallas_call(
        functools.partial(_inproj_kernel, tiles_per_seq=tiles_per_seq),
        grid=(n // t,),
        in_specs=[halo_prev, tok(d), halo_next, _const_spec((1, d)), _const_spec((d, D_IN_PAD)),
                  _const_spec((1, Q_LORA)), _const_spec((1, KV_LORA)),
                  _const_spec((Q_LORA, hp)), _const_spec((Q_LORA, hp)),
                  _const_spec((KV_LORA, hp)), _const_spec((KV_LORA, hp)),
                  pos, pos, _const_spec((SUBLANE, CONV_CH)), _const_spec((1, CONV_CH))],
        out_specs=[tok(SSM_D), tok(CONV_CH), tok(LANE), tok(hp), tok(hp), tok(hp)],
        out_shape=outs,
        compiler_params=_params(("parallel",)),
        name="inproj",
    )(x2, x2, x2, nw, win, qnw, kvnw, wqm, wqr, wk, wv, cos_t, sin_t, cw, cb)


def _softplus(x):
    return jnp.maximum(x, 0.0) + jnp.log1p(jnp.exp(-jnp.abs(x)))


def _dot_pieces(a, passes, fn):
    out = None
    rem = a
    for _ in range(passes):
        piece = rem.astype(BF16)
        rem = rem - piece.astype(F32)
        term = fn(piece)
        out = term if out is None else out + term
    return out


def _ssd_chunk(xbc, dt_raw, dtb_ref, a_ref, e64_ref, h_in, reverse, lane_off):
    L = CHUNK
    P = SSM_HEAD_DIM
    gw = SSM_HPG * P
    xs = xbc[:, :SSM_D].astype(F32)
    bm = xbc[:, SSM_D:SSM_D + SSM_GROUPS * SSM_STATE]
    cm = xbc[:, SSM_D + SSM_GROUPS * SSM_STATE:]
    dt = _softplus(dt_raw + dtb_ref[...])
    dta = dt * a_ref[...]
    row = lax.broadcasted_iota(jnp.int32, (L, L), 0)
    col = lax.broadcasted_iota(jnp.int32, (L, L), 1)
    mask = (col >= row) if reverse else (col <= row)
    tri = jnp.where(mask, 1.0, 0.0).astype(BF16)
    cum = _dot_pieces(dta, 3, lambda piece: _dot(tri, piece))
    cum_t = cum.T
    dt_t = dt.T
    last = 0 if reverse else L - 1
    exp_cum = jnp.exp(cum)
    w_state = dt * jnp.exp(cum[last:last + 1, :] - cum)
    scal64 = _dot_pieces(jnp.concatenate([w_state, exp_cum], axis=0), 2,
                         lambda piece: _dot(piece, e64_ref[...]))
    xw = (xs * scal64[:L]).astype(BF16)
    exp_cum64 = scal64[L:]
    exp_total64 = exp_cum64[last:last + 1, :]
    lane = lax.broadcasted_iota(jnp.int32, (L, 2 * P), 1)
    ys = []
    h_out = []
    for g in range(SSM_GROUPS):
        bg = bm[:, g * SSM_STATE:(g + 1) * SSM_STATE]
        cg = cm[:, g * SSM_STATE:(g + 1) * SSM_STATE]
        cb = _dot_nt(cg, bg)
        h_prev = h_in[g]
        y_off = _dot(cg, h_prev.astype(BF16)) * exp_cum64[:, g * gw:(g + 1) * gw]
        pairs = []
        for pr in range(SSM_HPG // 2):
            x_pair = xbc[:, g * gw + pr * 2 * P:g * gw + (pr + 1) * 2 * P]
            ms = []
            for q in range(2):
                hh = g * SSM_HPG + pr * 2 + q
                ln = lane_off + hh
                seg = cum[:, ln:ln + 1] - cum_t[ln:ln + 1, :]
                dec = jnp.exp(jnp.where(mask, seg, -jnp.inf))
                ms.append((cb * dec * dt_t[ln:ln + 1, :]).astype(BF16))
            y2 = _dot(jnp.concatenate(ms, axis=0), x_pair)
            pairs.append(jnp.where(lane < P, y2[:L], y2[L:]))
        ys.append(y_off + jnp.concatenate(pairs, axis=1))
        st = _dot(bg.astype(F32).T.astype(BF16), xw[:, g * gw:(g + 1) * gw])
        h_out.append(h_prev * exp_total64[:, g * gw:(g + 1) * gw] + st)
    return jnp.concatenate(ys, axis=1), xs, h_out


def _ssd_kernel(xf_ref, xb_ref, dtf_ref, dtb_ref, bias_ref, a_ref, dskip_ref,
                e64f_ref, e64b_ref, yf_ref, yb_ref, hf_ref, hb_ref):
    @pl.when(pl.program_id(1) == 0)
    def _():
        hf_ref[...] = jnp.zeros_like(hf_ref)
        hb_ref[...] = jnp.zeros_like(hb_ref)

    n_sub = xf_ref.shape[1] // CHUNK
    h_f = [hf_ref[g] for g in range(SSM_GROUPS)]
    h_b = [hb_ref[g] for g in range(SSM_GROUPS)]
    for i in range(n_sub):
        rows = slice(i * CHUNK, (i + 1) * CHUNK)
        y_f, xs_f, h_f = _ssd_chunk(xf_ref[0, rows, :], dtf_ref[0, rows, :], bias_ref, a_ref,
                                    e64f_ref, h_f, False, 0)
        yf_ref[0, rows, :] = y_f + xs_f * dskip_ref[...]
        rows = slice((n_sub - 1 - i) * CHUNK, (n_sub - i) * CHUNK)
        y_b, _, h_b = _ssd_chunk(xb_ref[0, rows, :], dtb_ref[0, rows, :], bias_ref, a_ref,
                                 e64b_ref, h_b, True, SSM_HEADS)
        yb_ref[0, rows, :] = y_b
    for g in range(SSM_GROUPS):
        hf_ref[g] = h_f[g]
        hb_ref[g] = h_b[g]


def _head_lane_expander(lane_off, width):
    src_lane = lax.broadcasted_iota(jnp.int32, (LANE, SSM_HEADS * width), 0)
    dst_head = lax.broadcasted_iota(jnp.int32, (LANE, SSM_HEADS * width), 1) // width
    return (src_lane == dst_head + lane_off).astype(BF16)


def _ssd(xbc, dt, bias, a_row, dskip):
    b, s, _ = xbc.shape
    expanders = [_head_lane_expander(off, SSM_HEAD_DIM) for off in (0, SSM_HEADS)]
    rows = SSD_CHUNKS_PER_STEP * CHUNK
    nc = s // rows
    chunk = lambda w, f: pl.BlockSpec((1, rows, w), lambda i, c: (i, f(c), 0))
    fwd = lambda c: c
    bwd = lambda c: nc - 1 - c
    hshape = (SSM_GROUPS, SSM_STATE, SSM_HPG * SSM_HEAD_DIM)
    return pl.pallas_call(
        _ssd_kernel,
        grid=(b, nc),
        in_specs=[chunk(CONV_CH, fwd), chunk(CONV_CH, bwd), chunk(LANE, fwd), chunk(LANE, bwd),
                  _const_spec((1, LANE)), _const_spec((1, LANE)), _const_spec((1, SSM_D))]
        + [_const_spec(e.shape) for e in expanders],
        out_specs=[chunk(SSM_D, fwd), chunk(SSM_D, bwd)],
        out_shape=[jax.ShapeDtypeStruct((b, s, SSM_D), F32)] * 2,
        scratch_shapes=[pltpu.VMEM(hshape, F32), pltpu.VMEM(hshape, F32)],
        compiler_params=_params(("parallel", "arbitrary")),
        name="ssd",
    )(xbc, xbc, dt, dt, bias, a_row, dskip, *expanders)


def _flash_kernel(q_ref, k_ref, v_ref, o_ref, *, tk):
    s_len = k_ref.shape[1]
    tq = q_ref.shape[1]
    sls = [slice(hh * HEAD_PAD, (hh + 1) * HEAD_PAD) for hh in range(2)]

    def body(j, carry):
        off = pl.multiple_of(j * tk, tk)
        new = []
        for sl, (m, acc) in zip(sls, carry):
            kj = k_ref[0, pl.ds(off, tk), sl]
            vj = v_ref[0, pl.ds(off, tk), sl]
            s = _dot_nt(q_ref[0, :, sl], kj)
            m_new = jnp.maximum(m, jnp.max(s, axis=1, keepdims=True))
            alpha = jnp.exp2(m - m_new)
            p = jnp.exp2(s - m_new).astype(BF16)
            new.append((m_new, acc * alpha + _dot(p, vj)))
        return tuple(new)

    m0 = jnp.full((tq, 1), -jnp.inf, F32)
    acc0 = jnp.zeros((tq, HEAD_PAD), F32)
    carry = lax.fori_loop(0, s_len // tk, body, ((m0, acc0), (m0, acc0)), unroll=ATTN_UNROLL)
    outs = [acc / acc[:, V_ONE_LANE:V_ONE_LANE + 1] for _, acc in carry]
    lane = lax.broadcasted_iota(jnp.int32, (tq, HEAD_PAD), 1)
    o_ref[0] = jnp.where(lane < V_DIM, outs[0], pltpu.roll(outs[1], V_DIM, 1)).astype(BF16)


def _flash(q, k, v):
    b, s, hp = q.shape
    tq = min(ATTN_TQ, s)
    tk = min(ATTN_TK, s)
    pairs = MLA_HEADS // 2
    pw = 2 * HEAD_PAD
    return pl.pallas_call(
        functools.partial(_flash_kernel, tk=tk),
        grid=(b, pairs, s // tq),
        in_specs=[pl.BlockSpec((1, tq, pw), lambda i, p, j: (i, j, p)),
                  pl.BlockSpec((1, s, pw), lambda i, p, j: (i, 0, p)),
                  pl.BlockSpec((1, s, pw), lambda i, p, j: (i, 0, p))],
        out_specs=pl.BlockSpec((1, tq, 2 * V_DIM), lambda i, p, j: (i, j, p)),
        out_shape=jax.ShapeDtypeStruct((b, s, MLA_D), BF16),
        compiler_params=_params(("parallel", "parallel", "arbitrary")),
        name="mla_flash",
    )(q, k, v)


def _cross_attn(h1, kv_ref, pre_w, wq_ref, wo_ref, post_w):
    hn = _rms(h1, pre_w).astype(BF16)
    q = _dot(hn, wq_ref[...]).astype(BF16)
    heads = []
    for hd in range(XA_HEADS):
        sl = slice(hd * XA_HEAD_DIM, (hd + 1) * XA_HEAD_DIM)
        kh = kv_ref[0, :, sl]
        vh = kv_ref[0, :, D_MODEL + hd * XA_HEAD_DIM:D_MODEL + (hd + 1) * XA_HEAD_DIM]
        s = _dot_nt(q[:, sl], kh)
        p = jnp.exp(s - jnp.max(s, axis=1, keepdims=True))
        l = jnp.sum(p, axis=1, keepdims=True)
        heads.append((_dot(p.astype(BF16), vh) / l).astype(BF16))
    o = jnp.concatenate(heads, axis=1)
    xa = _dot(o, wo_ref[...])
    return h1 + _rms(xa, post_w)


def _postmix_even_kernel(h_ref, yf_ref, yb_ref, z_ref, o_ref, snw_ref, wout_ref, mpost_ref,
                         kv_ref, xpre_ref, wq_ref, wo_ref, xpost_ref, out_ref):
    y = (yf_ref[0] + yb_ref[0]) * _silu(z_ref[0])
    gw = SSM_D // SSM_GROUPS
    parts = []
    for g in range(SSM_GROUPS):
        yg = y[:, g * gw:(g + 1) * gw]
        parts.append(yg * lax.rsqrt(jnp.mean(yg * yg, axis=-1, keepdims=True) + EPS))
    y_ssd = (jnp.concatenate(parts, axis=1) * snw_ref[...]).astype(BF16)
    mix = _dot(y_ssd, wout_ref[:SSM_D, :]) + _dot(o_ref[0], wout_ref[SSM_D:, :])
    h1 = h_ref[0] + _rms(mix, mpost_ref[...])
    out_ref[0] = _cross_attn(h1, kv_ref, xpre_ref[...], wq_ref, wo_ref, xpost_ref[...])


def _postmix_odd_kernel(h_ref, f_ref, wmix_ref, mpost_ref,
                        kv_ref, xpre_ref, wq_ref, wo_ref, xpost_ref, out_ref):
    mix = _dot(f_ref[0], wmix_ref[...])
    h1 = h_ref[0] + _rms(mix, mpost_ref[...])
    out_ref[0] = _cross_attn(h1, kv_ref, xpre_ref[...], wq_ref, wo_ref, xpost_ref[...])


def _postmix(kernel, h, mixed, mixed_w, consts_a, kv, kv_off, consts_b):
    b, s, d = h.shape
    t = TOKEN_TILE
    tok = lambda w: pl.BlockSpec((1, t, w), lambda i, j: (i, j, 0))
    in_specs = [tok(d)] + [tok(w) for w in mixed_w]
    in_specs += [_const_spec(c.shape) for c in consts_a]
    in_specs += [pl.BlockSpec((1, N_MEM, 2 * d), lambda i, j: (i + kv_off, 0, 0))]
    in_specs += [_const_spec(c.shape) for c in consts_b]
    return pl.pallas_call(
        kernel,
        grid=(b, s // t),
        in_specs=in_specs,
        out_specs=tok(d),
        out_shape=jax.ShapeDtypeStruct((b, s, d), F32),
        compiler_params=_params(("parallel", "parallel")),
        name="postmix",
    )(h, *mixed, *consts_a, kv, *consts_b)


def _ffn_kernel(h_ref, pre_ref, wg_ref, wu_ref, wd_ref, post_ref, *rest):
    out_ref = rest[-2] if len(rest) == 3 else rest[0]
    h = h_ref[...]
    hn = _rms(h, pre_ref[...]).astype(BF16)
    d_ff = wg_ref.shape[1]
    acc = jnp.zeros(h.shape, F32)
    for c in range(d_ff // FFN_CHUNK):
        sl = slice(c * FFN_CHUNK, (c + 1) * FFN_CHUNK)
        g = _dot(hn, wg_ref[:, sl])
        u = _dot(hn, wu_ref[:, sl])
        acc = acc + _dot((_silu(g) * u).astype(BF16), wd_ref[sl, :])
    out = h + _rms(acc, post_ref[...])
    out_ref[...] = out
    if len(rest) == 3:
        rest[2][...] = _rms(out, rest[0][...]).astype(BF16)


def _ffn(h2, pre, wg, wu, wd, post, next_pre=None):
    n, d = h2.shape
    t = TOKEN_TILE
    tok = pl.BlockSpec((t, d), lambda i: (i, 0))
    in_specs = [tok, _const_spec((1, d)), _const_spec(wg.shape), _const_spec(wu.shape),
                _const_spec(wd.shape), _const_spec((1, d))]
    args = [h2, pre, wg, wu, wd, post]
    out_specs, out_shape = tok, jax.ShapeDtypeStruct((n, d), F32)
    if next_pre is not None:
        in_specs.append(_const_spec((1, d)))
        args.append(next_pre)
        out_specs, out_shape = [tok, tok], [out_shape, jax.ShapeDtypeStruct((n, d), BF16)]
    return pl.pallas_call(
        _ffn_kernel,
        grid=(n // t,),
        in_specs=in_specs,
        out_specs=out_specs,
        out_shape=out_shape,
        compiler_params=_params(("parallel",)),
        name="ffn",
    )(*args)


def _fnet_a_kernel(x_ref, cs_ref, m1_ref, a_ref):
    gd = FOURIER_GROUP_DIM
    n1 = x_ref.shape[1]
    n_col = x_ref.shape[2]
    xt = pltpu.einshape("kcd->ckd", x_ref[0])
    slabs = [xt[c, :, g * gd:(g + 1) * gd] for c in range(n_col) for g in range(FOURIER_GROUPS)]
    y = _dot(jnp.concatenate(slabs, axis=0), cs_ref[...])
    outs = []
    for c in range(n_col):
        rows = [slice((c * FOURIER_GROUPS + g) * n1, (c * FOURIER_GROUPS + g + 1) * n1)
                for g in range(FOURIER_GROUPS)]
        yr = jnp.concatenate([y[r, :gd] for r in rows], axis=1)
        yi = jnp.concatenate([y[r, gd:] for r in rows], axis=1)
        stack = jnp.concatenate([yr, yi], axis=0).astype(BF16)
        outs.append(_dot(m1_ref[...], stack))
    a = pltpu.einshape("ckd->kcd", jnp.stack(outs, axis=0))
    a_ref[0] = a.reshape(2, n1, n_col, D_MODEL).astype(BF16)


def _fnet_b_kernel(a_ref, g_ref, f_ref):
    outs = []
    for c in range(g_ref.shape[0]):
        stack = jnp.concatenate([a_ref[0, 0, c], a_ref[0, 1, c]], axis=0)
        outs.append(_dot(g_ref[c], stack))
    f_ref[0] = pltpu.einshape("ckd->kcd", jnp.stack(outs, axis=0)).astype(BF16)


def _fnet(xn, cs, m1, gt):
    b, s, d = xn.shape
    n1 = m1.shape[0] // 2
    n2 = s // n1
    ca = min(FNET_COLS, n2)
    cb = min(FNET_COLS, n1)
    a = pl.pallas_call(
        _fnet_a_kernel,
        grid=(b, n2 // ca),
        in_specs=[pl.BlockSpec((1, n1, ca, d), lambda i, j: (i, 0, j, 0)),
                  _const_spec(cs.shape), _const_spec(m1.shape)],
        out_specs=pl.BlockSpec((1, 2, n1, ca, d), lambda i, j: (i, 0, 0, j, 0)),
        out_shape=jax.ShapeDtypeStruct((b, 2, n1, n2, d), BF16),
        compiler_params=_params(("parallel", "parallel")),
        name="fnet_a",
    )(xn.reshape(b, n1, n2, d), cs, m1)
    f = pl.pallas_call(
        _fnet_b_kernel,
        grid=(b, n1 // cb),
        in_specs=[pl.BlockSpec((1, 2, cb, n2, d), lambda i, j: (i, 0, j, 0, 0)),
                  pl.BlockSpec((cb, n2, 2 * n2), lambda i, j: (j, 0, 0))],
        out_specs=pl.BlockSpec((1, n2, cb, d), lambda i, j: (i, 0, j, 0)),
        out_shape=jax.ShapeDtypeStruct((b, n2, n1, d), BF16),
        compiler_params=_params(("parallel", "parallel")),
        name="fnet_b",
    )(a, gt)
    return f.reshape(b, s, d)


def _rope_tables(s):
    inv = ROPE_THETA ** (-jnp.arange(0, QK_ROPE, 2, dtype=F32) / QK_ROPE)
    ang = jnp.arange(s, dtype=F32)[:, None] * inv[None, :]
    cos2 = jnp.concatenate([jnp.cos(ang), jnp.cos(ang)], axis=1)
    sin2 = jnp.concatenate([jnp.sin(ang), jnp.sin(ang)], axis=1)
    pad = HEAD_PAD - QK_NOPE - QK_ROPE
    cos_t = jnp.concatenate([jnp.ones((s, QK_NOPE), F32), cos2, jnp.ones((s, pad), F32)], axis=1)
    sin_t = jnp.concatenate([jnp.zeros((s, QK_NOPE), F32), sin2, jnp.zeros((s, pad), F32)], axis=1)
    return cos_t, sin_t


def _rot_cols(w):
    half = w.shape[-1] // 2
    return jnp.concatenate([-w[..., half:], w[..., :half]], axis=-1)


def _pad_cols(w, left, total):
    return jnp.pad(w, ((0, 0), (left, total - left - w.shape[1])))


def _even_weights(w_in, w_uq, w_ukv):
    o1 = SSM_D
    o2 = o1 + CONV_CH
    o3 = o2 + 2 * SSM_HEADS
    o4 = o3 + Q_LORA
    o5 = o4 + KV_LORA
    w_z, w_xbc, w_dt, w_cq, w_ckv, w_kr = (w_in[:, :o1], w_in[:, o1:o2], w_in[:, o2:o3],
                                             w_in[:, o3:o4], w_in[:, o4:o5], w_in[:, o5:])
    win = jnp.concatenate([
        w_z, w_xbc, w_cq, w_ckv,
        _pad_cols(w_kr, QK_NOPE, LANE), _pad_cols(_rot_cols(w_kr), QK_NOPE, LANE),
        _pad_cols(w_dt, 0, LANE)], axis=1).astype(BF16)
    dq = QK_NOPE + QK_ROPE
    wq = w_uq.reshape(Q_LORA, MLA_HEADS, dq)
    zq = jnp.zeros((Q_LORA, MLA_HEADS, HEAD_PAD - dq), F32)
    wqm = jnp.concatenate([wq, zq], axis=-1).reshape(Q_LORA, -1).astype(BF16)
    wqr = jnp.concatenate([jnp.zeros((Q_LORA, MLA_HEADS, QK_NOPE), F32),
                           _rot_cols(wq[..., QK_NOPE:]), zq], axis=-1).reshape(Q_LORA, -1).astype(BF16)
    wkv = w_ukv.reshape(KV_LORA, MLA_HEADS, QK_NOPE + V_DIM)
    zk = jnp.zeros((KV_LORA, MLA_HEADS, HEAD_PAD - QK_NOPE), F32)
    wk = jnp.concatenate([wkv[..., :QK_NOPE], zk], axis=-1).reshape(KV_LORA, -1).astype(BF16)
    zv = jnp.zeros((KV_LORA, MLA_HEADS, HEAD_PAD - V_DIM), F32)
    wv = jnp.concatenate([wkv[..., QK_NOPE:], zv], axis=-1).reshape(KV_LORA, -1).astype(BF16)
    return win, wqm, wqr, wk, wv


def _fnet_tables(s):
    n2 = CHUNK
    n1 = s // n2
    gd = FOURIER_GROUP_DIM
    ci = jnp.arange(gd, dtype=jnp.int32)
    ang_c = (2.0 * math.pi / gd) * ((ci[:, None] * ci[None, :]) % gd).astype(F32)
    cs = (jnp.concatenate([jnp.cos(ang_c), -jnp.sin(ang_c)], axis=1) * gd ** -0.5).astype(BF16)
    i1 = jnp.arange(n1, dtype=jnp.int32)
    ang1 = (2.0 * math.pi / n1) * ((i1[:, None] * i1[None, :]) % n1).astype(F32)
    c1, s1 = jnp.cos(ang1), jnp.sin(ang1)
    m1 = jnp.concatenate([jnp.concatenate([c1, s1], axis=1),
                          jnp.concatenate([-s1, c1], axis=1)], axis=0).astype(BF16)
    i2 = jnp.arange(n2, dtype=jnp.int32)
    ang_a = (2.0 * math.pi / s) * (i1[:, None] * i2[None, :]).astype(F32)
    ang_b = (2.0 * math.pi / n2) * ((i2[:, None] * i2[None, :]) % n2).astype(F32)
    ca, sa = jnp.cos(ang_a)[:, None, :], jnp.sin(ang_a)[:, None, :]
    cb, sb = jnp.cos(ang_b)[None, :, :], jnp.sin(ang_b)[None, :, :]
    gt = (jnp.concatenate([ca * cb - sa * sb, sa * cb + ca * sb], axis=2) * s ** -0.5).astype(BF16)
    return cs, m1, gt


def _row(v, width=None):
    v = v.astype(F32).reshape(1, -1)
    if width is not None:
        v = jnp.pad(v, ((0, 0), (0, width - v.shape[1])))
    return v


def _trunk(x, kv_layers, kv_off, p):
    b, s, d = x.shape
    n = b * s
    h = x
    assert d == D_MODEL and s % min(ATTN_TQ, s) == 0 and s % min(ATTN_TK, s) == 0
    assert s % TOKEN_TILE == 0 and s % (SSD_CHUNKS_PER_STEP * CHUNK) == 0 and (s // CHUNK) % SUBLANE == 0
    z, xbc, dt, q, k, v = _inproj(h.reshape(n, d), p["mix_pre"][0], p["win"], p["q_norm"], p["kv_norm"],
                                  p["wqm"], p["wqr"], p["wk"], p["wv"], p["cos"][:s], p["sin"][:s],
                                  p["conv_w"], p["conv_b"], s)
    yf, yb = _ssd(xbc.reshape(b, s, -1), dt.reshape(b, s, -1), p["dt_bias"], p["a_row"], p["d_skip"])
    hp = MLA_HEADS * HEAD_PAD
    o = _flash(q.reshape(b, s, hp), k.reshape(b, s, hp), v.reshape(b, s, hp))
    h = _postmix(_postmix_even_kernel, h, (yf, yb, z.reshape(b, s, -1), o),
                 (SSM_D, SSM_D, SSM_D, MLA_D),
                 (p["ssm_norm"], p["w_out"], p["mix_post"][0]), kv_layers[0], kv_off,
                 (p["xa_pre"][0], p["xa_wq"][0], p["xa_wo"][0], p["xa_post"][0]))
    h, hn = _ffn(h.reshape(n, d), p["ffn_pre"][0], p["wg"][0], p["wu"][0], p["wd"][0],
                 p["ffn_post"][0], next_pre=p["mix_pre"][1])
    h = h.reshape(b, s, d)
    cs, m1, gt = _fnet_tables(s)
    f = _fnet(hn.reshape(b, s, d), cs, m1, gt)
    h = _postmix(_postmix_odd_kernel, h, (f,), (d,), (p["w_mix"], p["mix_post"][1]),
                 kv_layers[1], kv_off,
                 (p["xa_pre"][1], p["xa_wq"][1], p["xa_wo"][1], p["xa_post"][1]))
    h = _ffn(h.reshape(n, d), p["ffn_pre"][1], p["wg"][1], p["wu"][1], p["wd"][1],
             p["ffn_post"][1]).reshape(b, s, d)
    return h


def kernel(x_prompt, x_sample, mem_prompt, mem_sample, norm_mix_pre, norm_mix_post, norm_xa_pre, norm_xa_post, norm_mem, xa_wq, xa_wkv, xa_wo, norm_ffn_pre, norm_ffn_post, ffn_w_gu, ffn_w_down, ev_w_in, ev_conv_w, ev_conv_b, ev_a_log_f, ev_a_log_b, ev_dt_bias_f, ev_dt_bias_b, ev_d_skip, ev_ssm_norm, ev_q_norm, ev_w_uq, ev_kv_norm, ev_w_ukv, ev_w_out, od_w_mix):
    depth = norm_mix_pre.shape[0]
    assert depth == 2 and ev_w_in.shape[0] == 1 and od_w_mix.shape[0] == 1
    d_ff = ffn_w_down.shape[1]
    assert d_ff % FFN_CHUNK == 0
    s_max = max(x_prompt.shape[1], x_sample.shape[1])
    cos_t, sin_t = _rope_tables(s_max)
    win, wqm, wqr, wk, wv = _even_weights(ev_w_in[0], ev_w_uq[0], ev_w_ukv[0])
    rows = lambda w: [_row(w[i]) for i in range(depth)]
    p = {
        "mix_pre": rows(norm_mix_pre), "mix_post": rows(norm_mix_post),
        "xa_pre": rows(norm_xa_pre), "xa_post": rows(norm_xa_post),
        "ffn_pre": rows(norm_ffn_pre), "ffn_post": rows(norm_ffn_post),
        "xa_wq": [(xa_wq[i] * XA_SCALE).astype(BF16) for i in range(depth)],
        "xa_wo": [xa_wo[i].astype(BF16) for i in range(depth)],
        "wg": [ffn_w_gu[i, :, :d_ff].astype(BF16) for i in range(depth)],
        "wu": [ffn_w_gu[i, :, d_ff:].astype(BF16) for i in range(depth)],
        "wd": [ffn_w_down[i].astype(BF16) for i in range(depth)],
        "win": win, "wqm": wqm, "wqr": wqr, "wk": wk, "wv": wv,
        "q_norm": _row(ev_q_norm[0]), "kv_norm": _row(ev_kv_norm[0]),
        "cos": cos_t, "sin": sin_t,
        "conv_w": jnp.pad(ev_conv_w[0].astype(F32), ((0, SUBLANE - D_CONV), (0, 0))),
        "conv_b": _row(ev_conv_b[0]),
        "dt_bias": _row(jnp.concatenate([ev_dt_bias_f[0], ev_dt_bias_b[0]]), LANE),
        "a_row": _row(-jnp.exp(jnp.concatenate([ev_a_log_f[0], ev_a_log_b[0]]).astype(F32)), LANE),
        "d_skip": _row(jnp.repeat(ev_d_skip[0].astype(F32), SSM_HEAD_DIM)),
        "ssm_norm": _row(ev_ssm_norm[0]),
        "w_out": ev_w_out[0].astype(BF16),
        "w_mix": od_w_mix[0].astype(BF16),
    }
    mem = jnp.concatenate([mem_prompt, mem_sample], axis=0)
    kv_layers = [_memkv(mem, _row(norm_mem[i]), xa_wkv[i].astype(BF16)) for i in range(depth)]
    y_prompt = _trunk(x_prompt, kv_layers, 0, p)
    y_sample = _trunk(x_sample, kv_layers, x_prompt.shape[0], p)
    return (y_prompt, y_sample)
```

```python
import functools
import math

import jax
import jax.numpy as jnp
from jax import lax
from jax.experimental import pallas as pl
from jax.experimental.pallas import tpu as pltpu

F32 = jnp.float32
BF16 = jnp.bfloat16

EPS = 1e-6
D_MODEL = 1024
N_MEM = 256

SSM_HEADS = 8
SSM_HEAD_DIM = 64
SSM_D = SSM_HEADS * SSM_HEAD_DIM
SSM_GROUPS = 2
SSM_HPG = SSM_HEADS // SSM_GROUPS
SSM_STATE = 128
D_CONV = 5
CONV_CH = SSM_D + 2 * SSM_GROUPS * SSM_STATE
CHUNK = 128

MLA_HEADS = 8
QK_NOPE = 64
QK_ROPE = 32
V_DIM = 64
Q_LORA = 256
KV_LORA = 128
ROPE_THETA = 10000.0
MLA_D = MLA_HEADS * V_DIM

FOURIER_GROUPS = 4
FOURIER_GROUP_DIM = D_MODEL // FOURIER_GROUPS

XA_HEADS = 4
XA_HEAD_DIM = D_MODEL // XA_HEADS
XA_SCALE = XA_HEAD_DIM ** -0.5
assert math.frexp(XA_SCALE)[0] == 0.5

LANE = 128
SUBLANE = 8
HEAD_PAD = LANE
V_ONE_LANE = V_DIM
VMEM_LIMIT = 56 * 1024 * 1024

TOKEN_TILE = 512
FFN_CHUNK = 256
ATTN_TQ = 1024
ATTN_TK = 2048
ATTN_UNROLL = 4
FNET_COLS = 16
SSD_CHUNKS_PER_STEP = 8
POSTMIX_TILE = 1024
POSTMIX_PARTS = 2
LOG2E = 1.4426950408889634

O_Z = 0
O_XBC = O_Z + SSM_D
O_CQ = O_XBC + CONV_CH
O_CKV = O_CQ + Q_LORA
O_KA = O_CKV + KV_LORA
O_KB = O_KA + LANE
O_DT = O_KB + LANE
D_IN_PAD = O_DT + LANE


def _params(sem, vmem=VMEM_LIMIT):
    return pltpu.CompilerParams(dimension_semantics=sem, vmem_limit_bytes=vmem)


def _rms(x, w):
    ms = jnp.mean(x * x, axis=-1, keepdims=True)
    return x * lax.rsqrt(ms + EPS) * w


def _silu(x):
    return x / (1.0 + jnp.exp(-x))


def _dot(a, b):
    return jnp.dot(a, b, preferred_element_type=F32)


def _dot_nt(a, b):
    return lax.dot_general(a, b, (((1,), (1,)), ((), ())), preferred_element_type=F32)


def _const_spec(shape):
    nd = len(shape)
    return pl.BlockSpec(shape, lambda *_: (0,) * nd)


def _memkv_kernel(mem_ref, nw_ref, wkv_ref, kv_ref):
    xn = _rms(mem_ref[0], nw_ref[...]).astype(BF16)
    kv_ref[0] = _dot(xn, wkv_ref[...]).astype(BF16)


def _memkv(mem, nw, wkv):
    b, m, d = mem.shape
    return pl.pallas_call(
        _memkv_kernel,
        grid=(b,),
        in_specs=[pl.BlockSpec((1, m, d), lambda i: (i, 0, 0)),
                  _const_spec((1, d)),
                  _const_spec((d, 2 * d))],
        out_specs=pl.BlockSpec((1, m, 2 * d), lambda i: (i, 0, 0)),
        out_shape=jax.ShapeDtypeStruct((b, m, 2 * d), BF16),
        compiler_params=_params(("parallel",)),
        name="memkv",
    )(mem, nw, wkv)


def _inproj_kernel(xp_ref, x_ref, xn_ref, nw_ref, win_ref, qnw_ref, kvnw_ref, wqm_ref, wqr_ref,
                   wk_ref, wv_ref, cos_ref, sin_ref, cw_ref, cb_ref,
                   z_ref, xbc_ref, dt_ref, q_ref, k_ref, v_ref, *, tiles_per_seq):
    t = x_ref.shape[0]
    pos = pl.program_id(0) % tiles_per_seq
    x_ext = jnp.concatenate([xp_ref[...], x_ref[...], xn_ref[...]], axis=0)
    hn = _rms(x_ext, nw_ref[...]).astype(BF16)
    proj_ext = _dot(hn, win_ref[...])
    proj = proj_ext[SUBLANE:SUBLANE + t, :]
    z_ref[...] = proj[:, O_Z:O_XBC]
    dt_ref[...] = proj[:, O_DT:D_IN_PAD]
    row = lax.broadcasted_iota(jnp.int32, (t + 2 * SUBLANE, 1), 0)
    inside = ((row >= SUBLANE) | (pos > 0)) & ((row < SUBLANE + t) | (pos < tiles_per_seq - 1))
    xbc_ext = jnp.where(inside, proj_ext[:, O_XBC:O_CQ], 0.0)
    acc = cb_ref[...] + jnp.zeros((t, CONV_CH), F32)
    n_ext = t + 2 * SUBLANE
    for j in range(D_CONV):
        tap = pltpu.roll(xbc_ext, (D_CONV // 2 - j) % n_ext, 0)[SUBLANE:SUBLANE + t, :]
        acc = acc + tap * cw_ref[j:j + 1, :]
    xbc_ref[...] = _silu(acc).astype(BF16)
    cqn = _rms(proj[:, O_CQ:O_CKV], qnw_ref[...]).astype(BF16)
    ckvn = _rms(proj[:, O_CKV:O_KA], kvnw_ref[...]).astype(BF16)
    cos_t = cos_ref[...]
    sin_t = sin_ref[...]
    qscale = LOG2E * (QK_NOPE + QK_ROPE) ** -0.5
    cos_q = cos_t * qscale
    sin_q = sin_t * qscale
    qm = _dot(cqn, wqm_ref[...])
    qr = _dot(cqn, wqr_ref[...])
    kr = proj[:, O_KA:O_KB] * cos_t + proj[:, O_KB:O_DT] * sin_t
    km = _dot(ckvn, wk_ref[...])
    vm = _dot(ckvn, wv_ref[...])
    lane = lax.broadcasted_iota(jnp.int32, (1, HEAD_PAD), 1)
    one_col = jnp.where(lane == V_ONE_LANE, 1.0, 0.0).astype(F32)
    for h in range(MLA_HEADS):
        sl = slice(h * HEAD_PAD, (h + 1) * HEAD_PAD)
        q_ref[:, sl] = (qm[:, sl] * cos_q + qr[:, sl] * sin_q).astype(BF16)
        k_ref[:, sl] = (km[:, sl] + kr).astype(BF16)
        v_ref[:, sl] = (vm[:, sl] + one_col).astype(BF16)


def _inproj(x2, nw, win, qnw, kvnw, wqm, wqr, wk, wv, cos_t, sin_t, cw, cb, seq):
    n, d = x2.shape
    t = TOKEN_TILE
    tiles_per_seq = seq // t
    rb = t // SUBLANE
    nrb = n // SUBLANE
    tok = lambda w: pl.BlockSpec((t, w), lambda i: (i, 0))
    halo_prev = pl.BlockSpec((SUBLANE, d), lambda i: (jnp.maximum(i * rb - 1, 0), 0))
    halo_next = pl.BlockSpec((SUBLANE, d), lambda i: (jnp.minimum(i * rb + rb, nrb - 1), 0))
    pos = pl.BlockSpec((t, HEAD_PAD), lambda i: (i % tiles_per_seq, 0))
    hp = MLA_HEADS * HEAD_PAD
    outs = [jax.ShapeDtypeStruct((n, SSM_D), F32), jax.ShapeDtypeStruct((n, CONV_CH), BF16),
            jax.ShapeDtypeStruct((n, LANE), F32), jax.ShapeDtypeStruct((n, hp), BF16),
            jax.ShapeDtypeStruct((n, hp), BF16), jax.ShapeDtypeStruct((n, hp), BF16)]
    return pl.pallas_call(
        functools.partial(_inproj_kernel, tiles_per_seq=tiles_per_seq),
        grid=(n // t,),
        in_specs=[halo_prev, tok(d), halo_next, _const_spec((1, d)), _const_spec((d, D_IN_PAD)),
                  _const_spec((1, Q_LORA)), _const_spec((1, KV_LORA)),
                  _const_spec((Q_LORA, hp)), _const_spec((Q_LORA, hp)),
                  _const_spec((KV_LORA, hp)), _const_spec((KV_LORA, hp)),
                  pos, pos, _const_spec((SUBLANE, CONV_CH)), _const_spec((1, CONV_CH))],
        out_specs=[tok(SSM_D), tok(CONV_CH), tok(LANE), tok(hp), tok(hp), tok(hp)],
        out_shape=outs,
        compiler_params=_params(("parallel",)),
        name="inproj",
    )(x2, x2, x2, nw, win, qnw, kvnw, wqm, wqr, wk, wv, cos_t, sin_t, cw, cb)


def _softplus(x):
    return jnp.maximum(x, 0.0) + jnp.log1p(jnp.exp(-jnp.abs(x)))


def _dot_pieces(a, passes, fn):
    out = None
    rem = a
    for _ in range(passes):
        piece = rem.astype(BF16)
        rem = rem - piece.astype(F32)
        term = fn(piece)
        out = term if out is None else out + term
    return out


def _ssd_chunk(xbc, dt_raw, dtb_ref, a_ref, e64_ref, h_in, reverse, lane_off):
    L = CHUNK
    P = SSM_HEAD_DIM
    gw = SSM_HPG * P
    xs = xbc[:, :SSM_D].astype(F32)
    bm = xbc[:, SSM_D:SSM_D + SSM_GROUPS * SSM_STATE]
    cm = xbc[:, SSM_D + SSM_GROUPS * SSM_STATE:]
    dt = _softplus(dt_raw + dtb_ref[...])
    dta = dt * a_ref[...]
    row = lax.broadcasted_iota(jnp.int32, (L, L), 0)
    col = lax.broadcasted_iota(jnp.int32, (L, L), 1)
    mask = (col >= row) if reverse else (col <= row)
    tri = jnp.where(mask, 1.0, 0.0).astype(BF16)
    cum = _dot_pieces(dta, 3, lambda piece: _dot(tri, piece))
    cum_t = cum.T
    dt_t = dt.T
    last = 0 if reverse else L - 1
    exp_cum = jnp.exp(cum)
    w_state = dt * jnp.exp(cum[last:last + 1, :] - cum)
    scal64 = _dot_pieces(jnp.concatenate([w_state, exp_cum], axis=0), 2,
                         lambda piece: _dot(piece, e64_ref[...]))
    xw = (xs * scal64[:L]).astype(BF16)
    exp_cum64 = scal64[L:]
    exp_total64 = exp_cum64[last:last + 1, :]
    lane = lax.broadcasted_iota(jnp.int32, (L, 2 * P), 1)
    ys = []
    h_out = []
    for g in range(SSM_GROUPS):
        bg = bm[:, g * SSM_STATE:(g + 1) * SSM_STATE]
        cg = cm[:, g * SSM_STATE:(g + 1) * SSM_STATE]
        cb = _dot_nt(cg, bg)
        h_prev = h_in[g]
        y_off = _dot(cg, h_prev.astype(BF16)) * exp_cum64[:, g * gw:(g + 1) * gw]
        pairs = []
        for pr in range(SSM_HPG // 2):
            x_pair = xbc[:, g * gw + pr * 2 * P:g * gw + (pr + 1) * 2 * P]
            ms = []
            for q in range(2):
                hh = g * SSM_HPG + pr * 2 + q
                ln = lane_off + hh
                seg = cum[:, ln:ln + 1] - cum_t[ln:ln + 1, :]
                dec = jnp.exp(jnp.where(mask, seg, -jnp.inf))
                ms.append((cb * dec * dt_t[ln:ln + 1, :]).astype(BF16))
            y2 = _dot(jnp.concatenate(ms, axis=0), x_pair)
            pairs.append(jnp.where(lane < P, y2[:L], y2[L:]))
        ys.append(y_off + jnp.concatenate(pairs, axis=1))
        st = _dot(bg.astype(F32).T.astype(BF16), xw[:, g * gw:(g + 1) * gw])
        h_out.append(h_prev * exp_total64[:, g * gw:(g + 1) * gw] + st)
    return jnp.concatenate(ys, axis=1), xs, h_out


def _ssd_kernel(xf_ref, xb_ref, dtf_ref, dtb_ref, bias_ref, a_ref, dskip_ref,
                e64f_ref, e64b_ref, yf_ref, yb_ref, hf_ref, hb_ref):
    @pl.when(pl.program_id(1) == 0)
    def _():
        hf_ref[...] = jnp.zeros_like(hf_ref)
        hb_ref[...] = jnp.zeros_like(hb_ref)

    n_sub = xf_ref.shape[1] // CHUNK
    h_f = [hf_ref[g] for g in range(SSM_GROUPS)]
    h_b = [hb_ref[g] for g in range(SSM_GROUPS)]
    for i in range(n_sub):
        rows = slice(i * CHUNK, (i + 1) * CHUNK)
        y_f, xs_f, h_f = _ssd_chunk(xf_ref[0, rows, :], dtf_ref[0, rows, :], bias_ref, a_ref,
                                    e64f_ref, h_f, False, 0)
        yf_ref[0, rows, :] = y_f + xs_f * dskip_ref[...]
        rows = slice((n_sub - 1 - i) * CHUNK, (n_sub - i) * CHUNK)
        y_b, _, h_b = _ssd_chunk(xb_ref[0, rows, :], dtb_ref[0, rows, :], bias_ref, a_ref,
                                 e64b_ref, h_b, True, SSM_HEADS)
        yb_ref[0, rows, :] = y_b
    for g in range(SSM_GROUPS):
        hf_ref[g] = h_f[g]
        hb_ref[g] = h_b[g]


def _head_lane_expander(lane_off, width):
    src_lane = lax.broadcasted_iota(jnp.int32, (LANE, SSM_HEADS * width), 0)
    dst_head = lax.broadcasted_iota(jnp.int32, (LANE, SSM_HEADS * width), 1) // width
    return (src_lane == dst_head + lane_off).astype(BF16)


def _ssd(xbc, dt, bias, a_row, dskip):
    b, s, _ = xbc.shape
    expanders = [_head_lane_expander(off, SSM_HEAD_DIM) for off in (0, SSM_HEADS)]
    rows = SSD_CHUNKS_PER_STEP * CHUNK
    nc = s // rows
    chunk = lambda w, f: pl.BlockSpec((1, rows, w), lambda i, c: (i, f(c), 0))
    fwd = lambda c: c
    bwd = lambda c: nc - 1 - c
    hshape = (SSM_GROUPS, SSM_STATE, SSM_HPG * SSM_HEAD_DIM)
    return pl.pallas_call(
        _ssd_kernel,
        grid=(b, nc),
        in_specs=[chunk(CONV_CH, fwd), chunk(CONV_CH, bwd), chunk(LANE, fwd), chunk(LANE, bwd),
                  _const_spec((1, LANE)), _const_spec((1, LANE)), _const_spec((1, SSM_D))]
        + [_const_spec(e.shape) for e in expanders],
        out_specs=[chunk(SSM_D, fwd), chunk(SSM_D, bwd)],
        out_shape=[jax.ShapeDtypeStruct((b, s, SSM_D), F32)] * 2,
        scratch_shapes=[pltpu.VMEM(hshape, F32), pltpu.VMEM(hshape, F32)],
        compiler_params=_params(("parallel", "arbitrary")),
        name="ssd",
    )(xbc, xbc, dt, dt, bias, a_row, dskip, *expanders)


def _flash_kernel(q_ref, k_ref, v_ref, o_ref, *, tk):
    s_len = k_ref.shape[1]
    tq = q_ref.shape[1]
    sls = [slice(hh * HEAD_PAD, (hh + 1) * HEAD_PAD) for hh in range(2)]

    def body(j, carry):
        off = pl.multiple_of(j * tk, tk)
        new = []
        for sl, (m, acc) in zip(sls, carry):
            kj = k_ref[0, pl.ds(off, tk), sl]
            vj = v_ref[0, pl.ds(off, tk), sl]
            s = _dot_nt(q_ref[0, :, sl], kj)
            m_new = jnp.maximum(m, jnp.max(s, axis=1, keepdims=True))
            alpha = jnp.exp2(m - m_new)
            p = jnp.exp2(s - m_new).astype(BF16)
            new.append((m_new, acc * alpha + _dot(p, vj)))
        return tuple(new)

    m0 = jnp.full((tq, 1), -jnp.inf, F32)
    acc0 = jnp.zeros((tq, HEAD_PAD), F32)
    carry = lax.fori_loop(0, s_len // tk, body, ((m0, acc0), (m0, acc0)), unroll=ATTN_UNROLL)
    outs = [acc / acc[:, V_ONE_LANE:V_ONE_LANE + 1] for _, acc in carry]
    lane = lax.broadcasted_iota(jnp.int32, (tq, HEAD_PAD), 1)
    o_ref[0] = jnp.where(lane < V_DIM, outs[0], pltpu.roll(outs[1], V_DIM, 1)).astype(BF16)


def _flash(q, k, v):
    b, s, hp = q.shape
    tq = min(ATTN_TQ, s)
    tk = min(ATTN_TK, s)
    pairs = MLA_HEADS // 2
    pw = 2 * HEAD_PAD
    return pl.pallas_call(
        functools.partial(_flash_kernel, tk=tk),
        grid=(b, pairs, s // tq),
        in_specs=[pl.BlockSpec((1, tq, pw), lambda i, p, j: (i, j, p)),
                  pl.BlockSpec((1, s, pw), lambda i, p, j: (i, 0, p)),
                  pl.BlockSpec((1, s, pw), lambda i, p, j: (i, 0, p))],
        out_specs=pl.BlockSpec((1, tq, 2 * V_DIM), lambda i, p, j: (i, j, p)),
        out_shape=jax.ShapeDtypeStruct((b, s, MLA_D), BF16),
        compiler_params=_params(("parallel", "parallel", "arbitrary")),
        name="mla_flash",
    )(q, k, v)


def _cross_attn(h1s, kv_ref, pre_w, wq_ref, wo_ref, post_w):
    hns = [_rms(h1, pre_w).astype(BF16) for h1 in h1s]
    qs = [_dot(hn, wq_ref[...]).astype(BF16) for hn in hns]
    os_ = []
    for q in qs:
        heads = []
        for hd in range(XA_HEADS):
            sl = slice(hd * XA_HEAD_DIM, (hd + 1) * XA_HEAD_DIM)
            kh = kv_ref[0, :, sl]
            vh = kv_ref[0, :, D_MODEL + hd * XA_HEAD_DIM:D_MODEL + (hd + 1) * XA_HEAD_DIM]
            s = _dot_nt(q[:, sl], kh)
            p = jnp.exp(s - jnp.max(s, axis=1, keepdims=True))
            l = jnp.sum(p, axis=1, keepdims=True)
            heads.append((_dot(p.astype(BF16), vh) / l).astype(BF16))
        os_.append(jnp.concatenate(heads, axis=1))
    xas = [_dot(o, wo_ref[...]) for o in os_]
    return [h1 + _rms(xa, post_w) for h1, xa in zip(h1s, xas)]


def _row_parts(t):
    step = t // POSTMIX_PARTS
    return [slice(i * step, (i + 1) * step) for i in range(POSTMIX_PARTS)]


def _postmix_even_kernel(h_ref, yf_ref, yb_ref, z_ref, o_ref, snw_ref, wout_ref, mpost_ref,
                         kv_ref, xpre_ref, wq_ref, wo_ref, xpost_ref, out_ref):
    parts = _row_parts(h_ref.shape[1])
    gw = SSM_D // SSM_GROUPS
    y_ssds = []
    for rows in parts:
        y = (yf_ref[0, rows, :] + yb_ref[0, rows, :]) * _silu(z_ref[0, rows, :])
        gs = []
        for g in range(SSM_GROUPS):
            yg = y[:, g * gw:(g + 1) * gw]
            gs.append(yg * lax.rsqrt(jnp.mean(yg * yg, axis=-1, keepdims=True) + EPS))
        y_ssds.append((jnp.concatenate(gs, axis=1) * snw_ref[...]).astype(BF16))
    mixes = [_dot(y_ssd, wout_ref[:SSM_D, :]) + _dot(o_ref[0, rows, :], wout_ref[SSM_D:, :])
             for rows, y_ssd in zip(parts, y_ssds)]
    h1s = [h_ref[0, rows, :] + _rms(mix, mpost_ref[...]) for rows, mix in zip(parts, mixes)]
    outs = _cross_attn(h1s, kv_ref, xpre_ref[...], wq_ref, wo_ref, xpost_ref[...])
    for rows, out in zip(parts, outs):
        out_ref[0, rows, :] = out


def _postmix_odd_kernel(h_ref, f_ref, wmix_ref, mpost_ref,
                        kv_ref, xpre_ref, wq_ref, wo_ref, xpost_ref, out_ref):
    parts = _row_parts(h_ref.shape[1])
    mixes = [_dot(f_ref[0, rows, :], wmix_ref[...]) for rows in parts]
    h1s = [h_ref[0, rows, :] + _rms(mix, mpost_ref[...]) for rows, mix in zip(parts, mixes)]
    outs = _cross_attn(h1s, kv_ref, xpre_ref[...], wq_ref, wo_ref, xpost_ref[...])
    for rows, out in zip(parts, outs):
        out_ref[0, rows, :] = out


def _postmix(kernel, h, mixed, mixed_w, consts_a, kv, kv_off, consts_b):
    b, s, d = h.shape
    t = POSTMIX_TILE
    tok = lambda w: pl.BlockSpec((1, t, w), lambda i, j: (i, j, 0))
    in_specs = [tok(d)] + [tok(w) for w in mixed_w]
    in_specs += [_const_spec(c.shape) for c in consts_a]
    in_specs += [pl.BlockSpec((1, N_MEM, 2 * d), lambda i, j: (i + kv_off, 0, 0))]
    in_specs += [_const_spec(c.shape) for c in consts_b]
    return pl.pallas_call(
        kernel,
        grid=(b, s // t),
        in_specs=in_specs,
        out_specs=tok(d),
        out_shape=jax.ShapeDtypeStruct((b, s, d), F32),
        compiler_params=_params(("parallel", "parallel")),
        name="postmix",
    )(h, *mixed, *consts_a, kv, *consts_b)


def _ffn_kernel(h_ref, pre_ref, wg_ref, wu_ref, wd_ref, post_ref, *rest):
    out_ref = rest[-2] if len(rest) == 3 else rest[0]
    h = h_ref[...]
    hn = _rms(h, pre_ref[...]).astype(BF16)
    d_ff = wg_ref.shape[1]
    acc = jnp.zeros(h.shape, F32)
    for c in range(d_ff // FFN_CHUNK):
        sl = slice(c * FFN_CHUNK, (c + 1) * FFN_CHUNK)
        g = _dot(hn, wg_ref[:, sl])
        u = _dot(hn, wu_ref[:, sl])
        acc = acc + _dot((_silu(g) * u).astype(BF16), wd_ref[sl, :])
    out = h + _rms(acc, post_ref[...])
    out_ref[...] = out
    if len(rest) == 3:
        rest[2][...] = _rms(out, rest[0][...]).astype(BF16)


def _ffn(h2, pre, wg, wu, wd, post, next_pre=None):
    n, d = h2.shape
    t = TOKEN_TILE
    tok = pl.BlockSpec((t, d), lambda i: (i, 0))
    in_specs = [tok, _const_spec((1, d)), _const_spec(wg.shape), _const_spec(wu.shape),
                _const_spec(wd.shape), _const_spec((1, d))]
    args = [h2, pre, wg, wu, wd, post]
    out_specs, out_shape = tok, jax.ShapeDtypeStruct((n, d), F32)
    if next_pre is not None:
        in_specs.append(_const_spec((1, d)))
        args.append(next_pre)
        out_specs, out_shape = [tok, tok], [out_shape, jax.ShapeDtypeStruct((n, d), BF16)]
    return pl.pallas_call(
        _ffn_kernel,
        grid=(n // t,),
        in_specs=in_specs,
        out_specs=out_specs,
        out_shape=out_shape,
        compiler_params=_params(("parallel",)),
        name="ffn",
    )(*args)


def _fnet_a_kernel(x_ref, cs_ref, m1_ref, a_ref):
    gd = FOURIER_GROUP_DIM
    n1 = x_ref.shape[1]
    n_col = x_ref.shape[2]
    xt = pltpu.einshape("kcd->ckd", x_ref[0])
    slabs = [xt[c, :, g * gd:(g + 1) * gd] for c in range(n_col) for g in range(FOURIER_GROUPS)]
    y = _dot(jnp.concatenate(slabs, axis=0), cs_ref[...])
    outs = []
    for c in range(n_col):
        rows = [slice((c * FOURIER_GROUPS + g) * n1, (c * FOURIER_GROUPS + g + 1) * n1)
                for g in range(FOURIER_GROUPS)]
        yr = jnp.concatenate([y[r, :gd] for r in rows], axis=1)
        yi = jnp.concatenate([y[r, gd:] for r in rows], axis=1)
        stack = jnp.concatenate([yr, yi], axis=0).astype(BF16)
        outs.append(_dot(m1_ref[...], stack))
    a = pltpu.einshape("ckd->kcd", jnp.stack(outs, axis=0))
    a_ref[0] = a.reshape(2, n1, n_col, D_MODEL).astype(BF16)


def _fnet_b_kernel(a_ref, g_ref, f_ref):
    outs = []
    for c in range(g_ref.shape[0]):
        stack = jnp.concatenate([a_ref[0, 0, c], a_ref[0, 1, c]], axis=0)
        outs.append(_dot(g_ref[c], stack))
    f_ref[0] = pltpu.einshape("ckd->kcd", jnp.stack(outs, axis=0)).astype(BF16)


def _fnet(xn, cs, m1, gt):
    b, s, d = xn.shape
    n1 = m1.shape[0] // 2
    n2 = s // n1
    ca = min(FNET_COLS, n2)
    cb = min(FNET_COLS, n1)
    a = pl.pallas_call(
        _fnet_a_kernel,
        grid=(b, n2 // ca),
        in_specs=[pl.BlockSpec((1, n1, ca, d), lambda i, j: (i, 0, j, 0)),
                  _const_spec(cs.shape), _const_spec(m1.shape)],
        out_specs=pl.BlockSpec((1, 2, n1, ca, d), lambda i, j: (i, 0, 0, j, 0)),
        out_shape=jax.ShapeDtypeStruct((b, 2, n1, n2, d), BF16),
        compiler_params=_params(("parallel", "parallel")),
        name="fnet_a",
    )(xn.reshape(b, n1, n2, d), cs, m1)
    f = pl.pallas_call(
        _fnet_b_kernel,
        grid=(b, n1 // cb),
        in_specs=[pl.BlockSpec((1, 2, cb, n2, d), lambda i, j: (i, 0, j, 0, 0)),
                  pl.BlockSpec((cb, n2, 2 * n2), lambda i, j: (j, 0, 0))],
        out_specs=pl.BlockSpec((1, n2, cb, d), lambda i, j: (i, 0, j, 0)),
        out_shape=jax.ShapeDtypeStruct((b, n2, n1, d), BF16),
        compiler_params=_params(("parallel", "parallel")),
        name="fnet_b",
    )(a, gt)
    return f.reshape(b, s, d)


def _rope_tables(s):
    inv = ROPE_THETA ** (-jnp.arange(0, QK_ROPE, 2, dtype=F32) / QK_ROPE)
    ang = jnp.arange(s, dtype=F32)[:, None] * inv[None, :]
    cos2 = jnp.concatenate([jnp.cos(ang), jnp.cos(ang)], axis=1)
    sin2 = jnp.concatenate([jnp.sin(ang), jnp.sin(ang)], axis=1)
    pad = HEAD_PAD - QK_NOPE - QK_ROPE
    cos_t = jnp.concatenate([jnp.ones((s, QK_NOPE), F32), cos2, jnp.ones((s, pad), F32)], axis=1)
    sin_t = jnp.concatenate([jnp.zeros((s, QK_NOPE), F32), sin2, jnp.zeros((s, pad), F32)], axis=1)
    return cos_t, sin_t


def _rot_cols(w):
    half = w.shape[-1] // 2
    return jnp.concatenate([-w[..., half:], w[..., :half]], axis=-1)


def _pad_cols(w, left, total):
    return jnp.pad(w, ((0, 0), (left, total - left - w.shape[1])))


def _even_weights(w_in, w_uq, w_ukv):
    o1 = SSM_D
    o2 = o1 + CONV_CH
    o3 = o2 + 2 * SSM_HEADS
    o4 = o3 + Q_LORA
    o5 = o4 + KV_LORA
    w_z, w_xbc, w_dt, w_cq, w_ckv, w_kr = (w_in[:, :o1], w_in[:, o1:o2], w_in[:, o2:o3],
                                             w_in[:, o3:o4], w_in[:, o4:o5], w_in[:, o5:])
    win = jnp.concatenate([
        w_z, w_xbc, w_cq, w_ckv,
        _pad_cols(w_kr, QK_NOPE, LANE), _pad_cols(_rot_cols(w_kr), QK_NOPE, LANE),
        _pad_cols(w_dt, 0, LANE)], axis=1).astype(BF16)
    dq = QK_NOPE + QK_ROPE
    wq = w_uq.reshape(Q_LORA, MLA_HEADS, dq)
    zq = jnp.zeros((Q_LORA, MLA_HEADS, HEAD_PAD - dq), F32)
    wqm = jnp.concatenate([wq, zq], axis=-1).reshape(Q_LORA, -1).astype(BF16)
    wqr = jnp.concatenate([jnp.zeros((Q_LORA, MLA_HEADS, QK_NOPE), F32),
                           _rot_cols(wq[..., QK_NOPE:]), zq], axis=-1).reshape(Q_LORA, -1).astype(BF16)
    wkv = w_ukv.reshape(KV_LORA, MLA_HEADS, QK_NOPE + V_DIM)
    zk = jnp.zeros((KV_LORA, MLA_HEADS, HEAD_PAD - QK_NOPE), F32)
    wk = jnp.concatenate([wkv[..., :QK_NOPE], zk], axis=-1).reshape(KV_LORA, -1).astype(BF16)
    zv = jnp.zeros((KV_LORA, MLA_HEADS, HEAD_PAD - V_DIM), F32)
    wv = jnp.concatenate([wkv[..., QK_NOPE:], zv], axis=-1).reshape(KV_LORA, -1).astype(BF16)
    return win, wqm, wqr, wk, wv


def _fnet_tables(s):
    n2 = CHUNK
    n1 = s // n2
    gd = FOURIER_GROUP_DIM
    ci = jnp.arange(gd, dtype=jnp.int32)
    ang_c = (2.0 * math.pi / gd) * ((ci[:, None] * ci[None, :]) % gd).astype(F32)
    cs = (jnp.concatenate([jnp.cos(ang_c), -jnp.sin(ang_c)], axis=1) * gd ** -0.5).astype(BF16)
    i1 = jnp.arange(n1, dtype=jnp.int32)
    ang1 = (2.0 * math.pi / n1) * ((i1[:, None] * i1[None, :]) % n1).astype(F32)
    c1, s1 = jnp.cos(ang1), jnp.sin(ang1)
    m1 = jnp.concatenate([jnp.concatenate([c1, s1], axis=1),
                          jnp.concatenate([-s1, c1], axis=1)], axis=0).astype(BF16)
    i2 = jnp.arange(n2, dtype=jnp.int32)
    ang_a = (2.0 * math.pi / s) * (i1[:, None] * i2[None, :]).astype(F32)
    ang_b = (2.0 * math.pi / n2) * ((i2[:, None] * i2[None, :]) % n2).astype(F32)
    ca, sa = jnp.cos(ang_a)[:, None, :], jnp.sin(ang_a)[:, None, :]
    cb, sb = jnp.cos(ang_b)[None, :, :], jnp.sin(ang_b)[None, :, :]
    gt = (jnp.concatenate([ca * cb - sa * sb, sa * cb + ca * sb], axis=2) * s ** -0.5).astype(BF16)
    return cs, m1, gt


def _row(v, width=None):
    v = v.astype(F32).reshape(1, -1)
    if width is not None:
        v = jnp.pad(v, ((0, 0), (0, width - v.shape[1])))
    return v


def _trunk(x, kv_layers, kv_off, p):
    b, s, d = x.shape
    n = b * s
    h = x
    assert d == D_MODEL and s % min(ATTN_TQ, s) == 0 and s % min(ATTN_TK, s) == 0
    assert s % POSTMIX_TILE == 0 and s % TOKEN_TILE == 0 and s % (SSD_CHUNKS_PER_STEP * CHUNK) == 0 and (s // CHUNK) % SUBLANE == 0
    z, xbc, dt, q, k, v = _inproj(h.reshape(n, d), p["mix_pre"][0], p["win"], p["q_norm"], p["kv_norm"],
                                  p["wqm"], p["wqr"], p["wk"], p["wv"], p["cos"][:s], p["sin"][:s],
                                  p["conv_w"], p["conv_b"], s)
    yf, yb = _ssd(xbc.reshape(b, s, -1), dt.reshape(b, s, -1), p["dt_bias"], p["a_row"], p["d_skip"])
    hp = MLA_HEADS * HEAD_PAD
    o = _flash(q.reshape(b, s, hp), k.reshape(b, s, hp), v.reshape(b, s, hp))
    h = _postmix(_postmix_even_kernel, h, (yf, yb, z.reshape(b, s, -1), o),
                 (SSM_D, SSM_D, SSM_D, MLA_D),
                 (p["ssm_norm"], p["w_out"], p["mix_post"][0]), kv_layers[0], kv_off,
                 (p["xa_pre"][0], p["xa_wq"][0], p["xa_wo"][0], p["xa_post"][0]))
    h, hn = _ffn(h.reshape(n, d), p["ffn_pre"][0], p["wg"][0], p["wu"][0], p["wd"][0],
                 p["ffn_post"][0], next_pre=p["mix_pre"][1])
    h = h.reshape(b, s, d)
    cs, m1, gt = _fnet_tables(s)
    f = _fnet(hn.reshape(b, s, d), cs, m1, gt)
    h = _postmix(_postmix_odd_kernel, h, (f,), (d,), (p["w_mix"], p["mix_post"][1]),
                 kv_layers[1], kv_off,
                 (p["xa_pre"][1], p["xa_wq"][1], p["xa_wo"][1], p["xa_post"][1]))
    h = _ffn(h.reshape(n, d), p["ffn_pre"][1], p["wg"][1], p["wu"][1], p["wd"][1],
             p["ffn_post"][1]).reshape(b, s, d)
    return h


def kernel(x_prompt, x_sample, mem_prompt, mem_sample, norm_mix_pre, norm_mix_post, norm_xa_pre, norm_xa_post, norm_mem, xa_wq, xa_wkv, xa_wo, norm_ffn_pre, norm_ffn_post, ffn_w_gu, ffn_w_down, ev_w_in, ev_conv_w, ev_conv_b, ev_a_log_f, ev_a_log_b, ev_dt_bias_f, ev_dt_bias_b, ev_d_skip, ev_ssm_norm, ev_q_norm, ev_w_uq, ev_kv_norm, ev_w_ukv, ev_w_out, od_w_mix):
    depth = norm_mix_pre.shape[0]
    assert depth == 2 and ev_w_in.shape[0] == 1 and od_w_mix.shape[0] == 1
    d_ff = ffn_w_down.shape[1]
    assert d_ff % FFN_CHUNK == 0
    s_max = max(x_prompt.shape[1], x_sample.shape[1])
    cos_t, sin_t = _rope_tables(s_max)
    win, wqm, wqr, wk, wv = _even_weights(ev_w_in[0], ev_w_uq[0], ev_w_ukv[0])
    rows = lambda w: [_row(w[i]) for i in range(depth)]
    p = {
        "mix_pre": rows(norm_mix_pre), "mix_post": rows(norm_mix_post),
        "xa_pre": rows(norm_xa_pre), "xa_post": rows(norm_xa_post),
        "ffn_pre": rows(norm_ffn_pre), "ffn_post": rows(norm_ffn_post),
        "xa_wq": [(xa_wq[i] * XA_SCALE).astype(BF16) for i in range(depth)],
        "xa_wo": [xa_wo[i].astype(BF16) for i in range(depth)],
        "wg": [ffn_w_gu[i, :, :d_ff].astype(BF16) for i in range(depth)],
        "wu": [ffn_w_gu[i, :, d_ff:].astype(BF16) for i in range(depth)],
        "wd": [ffn_w_down[i].astype(BF16) for i in range(depth)],
        "win": win, "wqm": wqm, "wqr": wqr, "wk": wk, "wv": wv,
        "q_norm": _row(ev_q_norm[0]), "kv_norm": _row(ev_kv_norm[0]),
        "cos": cos_t, "sin": sin_t,
        "conv_w": jnp.pad(ev_conv_w[0].astype(F32), ((0, SUBLANE - D_CONV), (0, 0))),
        "conv_b": _row(ev_conv_b[0]),
        "dt_bias": _row(jnp.concatenate([ev_dt_bias_f[0], ev_dt_bias_b[0]]), LANE),
        "a_row": _row(-jnp.exp(jnp.concatenate([ev_a_log_f[0], ev_a_log_b[0]]).astype(F32)), LANE),
        "d_skip": _row(jnp.repeat(ev_d_skip[0].astype(F32), SSM_HEAD_DIM)),
        "ssm_norm": _row(ev_ssm_norm[0]),
        "w_out": ev_w_out[0].astype(BF16),
        "w_mix": od_w_mix[0].astype(BF16),
    }
    mem = jnp.concatenate([mem_prompt, mem_sample], axis=0)
    kv_layers = [_memkv(mem, _row(norm_mem[i]), xa_wkv[i].astype(BF16)) for i in range(depth)]
    y_prompt = _trunk(x_prompt, kv_layers, 0, p)
    y_sample = _trunk(x_sample, kv_layers, x_prompt.shape[0], p)
    return (y_prompt, y_sample)
```
